```python
import math
import jax, jax.numpy as jnp
from jax import lax
import numpy as np

D_MODEL = 1024
BATCH = 32
SEQ = 2048
DEPTH = 4

ATTN_WIDTH = D_MODEL // 2
SSM_WIDTH = D_MODEL - ATTN_WIDTH
HEAD_DIM = 64
N_HEADS = ATTN_WIDTH // HEAD_DIM
DILATION_PAIRS = ((128, 1), (512, 4), (2048, 16))
ROPE_THETA = 10000.0
SSM_GROUP_DIM = 16
SSM_GROUPS = SSM_WIDTH // SSM_GROUP_DIM
SSM_STATE = 64
DT_MIN = 0.001
DT_MAX = 0.1
D_FF = 4 * D_MODEL
IN_WIDTH = 3 * ATTN_WIDTH + SSM_WIDTH
LN_EPS = 1e-5
RMS_EPS = 1e-6
NEG_INF = -1e30
DEEPNORM_ALPHA = (2.0 * DEPTH) ** 0.25
DEEPNORM_BETA = (8.0 * DEPTH) ** -0.25

kernel_name = "hymba_s5_dilated_attn_deepnorm_trunk"


def layer_norm(x, g, b):
    xf = x.astype(jnp.float32)
    mu = jnp.mean(xf, axis=-1, keepdims=True)
    var = jnp.mean(jnp.square(xf - mu), axis=-1, keepdims=True)
    y = (xf - mu) * lax.rsqrt(var + LN_EPS) * g.astype(jnp.float32) + b.astype(jnp.float32)
    return y.astype(x.dtype)


def rms_norm(x, g):
    xf = x.astype(jnp.float32)
    y = xf * lax.rsqrt(jnp.mean(jnp.square(xf), axis=-1, keepdims=True) + RMS_EPS)
    return y * g.astype(jnp.float32)


def rope(t, positions):
    half = t.shape[-1] // 2
    inv_freq = ROPE_THETA ** (-jnp.arange(half, dtype=jnp.float32) * 2.0 / t.shape[-1])
    ang = positions.astype(jnp.float32)[..., None] * inv_freq
    cos = jnp.cos(ang)[:, :, None, :]
    sin = jnp.sin(ang)[:, :, None, :]
    t1, t2 = t[..., :half], t[..., half:]
    return jnp.concatenate([t1 * cos - t2 * sin, t1 * sin + t2 * cos], axis=-1)


def dilated_branch(q, k, v, window, dilation):
    bsz, s, h, e = q.shape
    nk = window // dilation
    qb_size = nk
    sub_len = s // dilation
    nb = -(-sub_len // qb_size)
    padded = nb * qb_size
    pad = padded - sub_len

    def to_sub(t):
        return t.reshape(bsz, sub_len, dilation, h, e).transpose(0, 2, 3, 1, 4)

    qs = jnp.pad(to_sub(q), ((0, 0), (0, 0), (0, 0), (0, pad), (0, 0)))
    ks = jnp.pad(to_sub(k), ((0, 0), (0, 0), (0, 0), (qb_size, pad), (0, 0)))
    vs = jnp.pad(to_sub(v), ((0, 0), (0, 0), (0, 0), (qb_size, pad), (0, 0)))
    qblk = qs.reshape(bsz, dilation, h, nb, qb_size, e)
    kb = ks.reshape(bsz, dilation, h, nb + 1, qb_size, e)
    vb = vs.reshape(bsz, dilation, h, nb + 1, qb_size, e)
    keys = jnp.concatenate([kb[:, :, :, :-1], kb[:, :, :, 1:]], axis=4)
    vals = jnp.concatenate([vb[:, :, :, :-1], vb[:, :, :, 1:]], axis=4)

    scores = jnp.einsum('bdhnqe,bdhnke->bdhnqk', qblk, keys)
    qi = jnp.arange(qb_size)[:, None]
    kj = jnp.arange(2 * qb_size)[None, :]
    dist = qi + qb_size - kj
    key_idx = jnp.arange(nb)[:, None, None] * qb_size - qb_size + kj[None]
    valid = (dist >= 0)[None] & (dist <= nk)[None] & (key_idx >= 0)
    scores = jnp.where(valid, scores, NEG_INF)
    m = jnp.max(scores, axis=-1, keepdims=True)
    p = jnp.exp(scores - m)
    den = jnp.sum(p, axis=-1, keepdims=True)
    o = jnp.einsum('bdhnqk,bdhnke->bdhnqe', p, vals) / den
    lse = (m + jnp.log(den))[..., 0]

    o = o.reshape(bsz, dilation, h, padded, e)[:, :, :, :sub_len]
    o = o.transpose(0, 3, 1, 2, 4).reshape(bsz, s, h, e)
    lse = lse.reshape(bsz, dilation, h, padded)[:, :, :, :sub_len]
    lse = lse.transpose(0, 3, 1, 2).reshape(bsz, s, h)
    return o, lse


def dilated_attention(q, k, v, positions):
    bsz, s, _ = q.shape
    q = rope(q.astype(jnp.float32).reshape(bsz, s, N_HEADS, HEAD_DIM), positions)
    k = rope(k.astype(jnp.float32).reshape(bsz, s, N_HEADS, HEAD_DIM), positions)
    v = v.astype(jnp.float32).reshape(bsz, s, N_HEADS, HEAD_DIM)
    q = q * (HEAD_DIM ** -0.5)
    outs, lses = [], []
    for window, dilation in DILATION_PAIRS:
        o, lse = dilated_branch(q, k, v, window, dilation)
        outs.append(o)
        lses.append(lse)
    w = jax.nn.softmax(jnp.stack(lses, axis=0), axis=0)
    o = jnp.sum(w[..., None] * jnp.stack(outs, axis=0), axis=0)
    return o.reshape(bsz, s, ATTN_WIDTH)


def _ssm_combine(e1, e2):
    a1r, a1i, b1r, b1i = e1
    a2r, a2i, b2r, b2i = e2
    ar = a2r * a1r - a2i * a1i
    ai = a2r * a1i + a2i * a1r
    br = a2r * b1r - a2i * b1i + b2r
    bi = a2r * b1i + a2i * b1r + b2i
    return (ar, ai, br, bi)


def s5_mixer(u, a_re, a_im, log_dt, b_re, b_im, c_re, c_im, d_skip, w_glu, b_glu):
    bsz, s, _ = u.shape
    f32 = jnp.float32
    uf = u.astype(f32).reshape(bsz, s, SSM_GROUPS, SSM_GROUP_DIM)
    a_re = a_re.astype(f32)
    a_im = a_im.astype(f32)
    dt = jnp.exp(log_dt.astype(f32))[:, None]
    mag = jnp.exp(a_re * dt)
    ang = a_im * dt
    lb_re = mag * jnp.cos(ang)
    lb_im = mag * jnp.sin(ang)
    den = a_re * a_re + a_im * a_im
    nr = lb_re - 1.0
    ni = lb_im
    cr = (nr * a_re + ni * a_im) / den
    ci = (ni * a_re - nr * a_im) / den
    b_re = b_re.astype(f32)
    b_im = b_im.astype(f32)
    bb_re = cr[..., None] * b_re - ci[..., None] * b_im
    bb_im = cr[..., None] * b_im + ci[..., None] * b_re
    bu_re = jnp.einsum('bsgn,gpn->bsgp', uf, bb_re)
    bu_im = jnp.einsum('bsgn,gpn->bsgp', uf, bb_im)
    lam_re = jnp.broadcast_to(lb_re[None, None], (1, s, SSM_GROUPS, SSM_STATE))
    lam_im = jnp.broadcast_to(lb_im[None, None], (1, s, SSM_GROUPS, SSM_STATE))
    _, _, xr, xi = lax.associative_scan(_ssm_combine, (lam_re, lam_im, bu_re, bu_im), axis=1)
    y = (jnp.einsum('bsgp,gnp->bsgn', xr, c_re.astype(f32))
         - jnp.einsum('bsgp,gnp->bsgn', xi, c_im.astype(f32))
         + d_skip.astype(f32) * uf)
    y = jax.nn.gelu(y.reshape(bsz, s, SSM_WIDTH))
    y = y * jax.nn.sigmoid(y @ w_glu.astype(f32) + b_glu.astype(f32))
    return y


def setup_inputs(seed: int = 0) -> dict:
    key = jax.random.key(seed)
    ks = jax.random.split(key, 32)
    L = DEPTH
    nrm = jax.random.normal
    f32 = jnp.float32
    x = nrm(ks[0], (BATCH, SEQ, D_MODEL), f32)
    positions = (jax.random.randint(ks[1], (BATCH, 1), 0, 1024, dtype=jnp.int32)
                 + jnp.arange(SEQ, dtype=jnp.int32)[None, :])
    w_in = nrm(ks[2], (L, D_MODEL, IN_WIDTH), f32) * D_MODEL ** -0.5
    attn_gain = 1.0 + 0.02 * nrm(ks[3], (L, ATTN_WIDTH), f32)
    ssm_gain = 1.0 + 0.02 * nrm(ks[4], (L, SSM_WIDTH), f32)
    ssm_a_re = -0.5 + 0.01 * nrm(ks[5], (L, SSM_GROUPS, SSM_STATE), f32)
    ssm_a_im = (math.pi * jnp.arange(SSM_STATE, dtype=f32)[None, None, :]
                + 0.01 * nrm(ks[6], (L, SSM_GROUPS, SSM_STATE), f32))
    ssm_log_dt = jax.random.uniform(ks[7], (L, SSM_GROUPS), f32,
                                    math.log(DT_MIN), math.log(DT_MAX))
    bs = (2.0 * SSM_GROUP_DIM) ** -0.5
    cs = (2.0 * SSM_STATE) ** -0.5
    ssm_b_re = nrm(ks[8], (L, SSM_GROUPS, SSM_STATE, SSM_GROUP_DIM), f32) * bs
    ssm_b_im = nrm(ks[9], (L, SSM_GROUPS, SSM_STATE, SSM_GROUP_DIM), f32) * bs
    ssm_c_re = nrm(ks[10], (L, SSM_GROUPS, SSM_GROUP_DIM, SSM_STATE), f32) * cs
    ssm_c_im = nrm(ks[11], (L, SSM_GROUPS, SSM_GROUP_DIM, SSM_STATE), f32) * cs
    ssm_d = nrm(ks[12], (L, SSM_GROUPS, SSM_GROUP_DIM), f32)
    w_glu = nrm(ks[13], (L, SSM_WIDTH, SSM_WIDTH), f32) * SSM_WIDTH ** -0.5
    b_glu = 0.01 * nrm(ks[14], (L, SSM_WIDTH), f32)
    w_out = nrm(ks[15], (L, D_MODEL, D_MODEL), f32) * D_MODEL ** -0.5 * DEEPNORM_BETA
    b_out = 0.01 * nrm(ks[16], (L, D_MODEL), f32)
    ln1_g = 1.0 + 0.02 * nrm(ks[17], (L, D_MODEL), f32)
    ln1_b = 0.01 * nrm(ks[18], (L, D_MODEL), f32)
    w_ff1 = nrm(ks[19], (L, D_MODEL, D_FF), f32) * D_MODEL ** -0.5
    b_ff1 = 0.01 * nrm(ks[20], (L, D_FF), f32)
    w_ff2 = nrm(ks[21], (L, D_FF, D_MODEL), f32) * D_FF ** -0.5 * DEEPNORM_BETA
    b_ff2 = 0.01 * nrm(ks[22], (L, D_MODEL), f32)
    ln2_g = 1.0 + 0.02 * nrm(ks[23], (L, D_MODEL), f32)
    ln2_b = 0.01 * nrm(ks[24], (L, D_MODEL), f32)
    return {"x": x, "positions": positions, "w_in": w_in, "attn_gain": attn_gain,
            "ssm_gain": ssm_gain, "ssm_a_re": ssm_a_re, "ssm_a_im": ssm_a_im,
            "ssm_log_dt": ssm_log_dt, "ssm_b_re": ssm_b_re, "ssm_b_im": ssm_b_im,
            "ssm_c_re": ssm_c_re, "ssm_c_im": ssm_c_im, "ssm_d": ssm_d,
            "w_glu": w_glu, "b_glu": b_glu, "w_out": w_out, "b_out": b_out,
            "ln1_g": ln1_g, "ln1_b": ln1_b, "w_ff1": w_ff1, "b_ff1": b_ff1,
            "w_ff2": w_ff2, "b_ff2": b_ff2, "ln2_g": ln2_g, "ln2_b": ln2_b}


def reference(x, positions, w_in, attn_gain, ssm_gain, ssm_a_re, ssm_a_im, ssm_log_dt,
              ssm_b_re, ssm_b_im, ssm_c_re, ssm_c_im, ssm_d, w_glu, b_glu, w_out, b_out,
              ln1_g, ln1_b, w_ff1, b_ff1, w_ff2, b_ff2, ln2_g, ln2_b):
    h = x
    for l in range(DEPTH):
        proj = h @ w_in[l]
        q = proj[..., :ATTN_WIDTH]
        k = proj[..., ATTN_WIDTH:2 * ATTN_WIDTH]
        v = proj[..., 2 * ATTN_WIDTH:3 * ATTN_WIDTH]
        u = proj[..., 3 * ATTN_WIDTH:]
        attn = dilated_attention(q, k, v, positions)
        ssm = s5_mixer(u, ssm_a_re[l], ssm_a_im[l], ssm_log_dt[l], ssm_b_re[l], ssm_b_im[l],
                       ssm_c_re[l], ssm_c_im[l], ssm_d[l], w_glu[l], b_glu[l])
        mixed = jnp.concatenate([rms_norm(attn, attn_gain[l]), rms_norm(ssm, ssm_gain[l])],
                                axis=-1).astype(h.dtype)
        mix_out = mixed @ w_out[l] + b_out[l]
        h = layer_norm(DEEPNORM_ALPHA * h + mix_out, ln1_g[l], ln1_b[l])
        ff = jnp.square(jax.nn.relu(h @ w_ff1[l] + b_ff1[l])) @ w_ff2[l] + b_ff2[l]
        h = layer_norm(DEEPNORM_ALPHA * h + ff, ln2_g[l], ln2_b[l])
    return h
```

```python
import functools
import math

import jax
import jax.numpy as jnp
from jax import lax
from jax.experimental import pallas as pl
from jax.experimental.pallas import tpu as pltpu

F32 = jnp.float32
BF16 = jnp.bfloat16

HEAD_DIM = 64
LANES = 128
DILATION_PAIRS = ((128, 1), (512, 4), (2048, 16))
QBLK = 128
ROPE_THETA = 10000.0
SSM_GROUP_DIM = 16
SSM_STATE = 64
LN_EPS = 1e-5
RMS_EPS = 1e-6
NEG_INF = -1e30
VMEM_LIMIT = 56 * 1024 * 1024

ROW_TILE = 512
SCAN_STEPS = 16
FF_CHUNK = 1024


def _dot(a, b):
    return jnp.dot(a, b, preferred_element_type=F32)


def _dot_nt(a, b):
    return lax.dot_general(a, b, (((1,), (1,)), ((), ())), preferred_element_type=F32)


def _rope_table_kernel(pos_ref, freq_ref, sign_ref, cos_ref, sin_ref):
    ang = pos_ref[...] * freq_ref[...]
    cos_ref[...] = jnp.cos(ang)
    sin_ref[...] = jnp.sin(ang) * sign_ref[...]


def _rope_tables(pos_col, freq_row, sign_row):
    t = pos_col.shape[0]
    tm = 2048
    row = pl.BlockSpec((1, LANES), lambda i: (0, 0))
    tab = pl.BlockSpec((tm, LANES), lambda i: (i, 0))
    return pl.pallas_call(
        _rope_table_kernel,
        grid=(t // tm,),
        in_specs=[pl.BlockSpec((tm, 1), lambda i: (i, 0)), row, row],
        out_specs=[tab, tab],
        out_shape=[jax.ShapeDtypeStruct((t, LANES), F32)] * 2,
    )(pos_col, freq_row, sign_row)


def _proj_kernel(h_ref, w_ref, cos_ref, sin_ref, q_ref, k_ref, v_ref, u_ref, *, aw):
    hb = h_ref[...].astype(BF16)
    cos = cos_ref[...]
    sin = sin_ref[...]
    lane = lax.broadcasted_iota(jnp.int32, (1, LANES), 1)
    first_half = (lane % HEAD_DIM) < (HEAD_DIM // 2)

    def rope(x):
        rot = jnp.where(first_half,
                        pltpu.roll(x, LANES - HEAD_DIM // 2, 1),
                        pltpu.roll(x, HEAD_DIM // 2, 1))
        return x * cos + rot * sin

    q = _dot(hb, w_ref[:, 0:aw])
    k = _dot(hb, w_ref[:, aw:2 * aw])
    scale = HEAD_DIM ** -0.5
    for j in range(aw // LANES):
        sl = slice(j * LANES, (j + 1) * LANES)
        q_ref[:, sl] = (rope(q[:, sl]) * scale).astype(BF16)
        k_ref[:, sl] = rope(k[:, sl]).astype(BF16)
    v_ref[...] = _dot(hb, w_ref[:, 2 * aw:3 * aw]).astype(BF16)
    u_ref[...] = _dot(hb, w_ref[:, 3 * aw:]).astype(BF16)


def _proj(h, w_in, cos_t, sin_t, aw):
    t, d = h.shape
    n = w_in.shape[1]
    sw = n - 3 * aw
    tm = ROW_TILE
    rows = lambda width: pl.BlockSpec((tm, width), lambda i: (i, 0))
    return pl.pallas_call(
        functools.partial(_proj_kernel, aw=aw),
        grid=(t // tm,),
        in_specs=[rows(d), pl.BlockSpec((d, n), lambda i: (0, 0)), rows(LANES), rows(LANES)],
        out_specs=[rows(aw), rows(aw), rows(aw), rows(sw)],
        out_shape=[jax.ShapeDtypeStruct((t, aw), BF16)] * 3 + [jax.ShapeDtypeStruct((t, sw), BF16)],
        compiler_params=pltpu.CompilerParams(vmem_limit_bytes=VMEM_LIMIT),
    )(h, w_in, cos_t, sin_t)


def _attn_kernel(q_ref, k_ref, v_ref, o_ref, qf, kf, vf, ob, lb, bias_band, bias_causal):
    seq = q_ref.shape[0]
    qf[...] = q_ref[...].astype(F32)
    kf[...] = k_ref[...].astype(F32)
    vf[...] = v_ref[...].astype(F32)

    qi = lax.broadcasted_iota(jnp.int32, (QBLK, 2 * QBLK), 0)
    kj = lax.broadcasted_iota(jnp.int32, (QBLK, 2 * QBLK), 1)
    bias_band[...] = jnp.where((kj >= qi) & (kj <= qi + QBLK), 0.0, NEG_INF).astype(F32)
    qi = lax.broadcasted_iota(jnp.int32, (QBLK, QBLK), 0)
    kj = lax.broadcasted_iota(jnp.int32, (QBLK, QBLK), 1)
    bias_causal[...] = jnp.where(kj <= qi, 0.0, NEG_INF).astype(F32)

    head0 = lax.broadcasted_iota(jnp.int32, (1, LANES), 1) < HEAD_DIM

    def unit(qs, ks, vs, bias):
        qb = qs.astype(BF16)
        kb = ks.astype(BF16)
        vb = vs.astype(BF16)
        zero = jnp.zeros_like(qb)
        outs, lses = [], []
        for a in range(2):
            sel = head0 if a == 0 else jnp.logical_not(head0)
            s = _dot_nt(jnp.where(sel, qb, zero), kb) + bias
            m = jnp.max(s, axis=-1, keepdims=True)
            p = jnp.exp(s - m)
            den = jnp.sum(p, axis=-1, keepdims=True)
            outs.append(_dot(p.astype(BF16), vb) / den)
            lses.append(jnp.broadcast_to(m + jnp.log(den), (QBLK, LANES)))
        return jnp.where(head0, outs[0], outs[1]), jnp.where(head0, lses[0], lses[1])

    def branch(idx, dilation):
        n_sub_blocks = seq // dilation // QBLK

        def rows(start_sub, count, r):
            start = start_sub * dilation + r
            if dilation == 1:
                return pl.ds(start, count)
            return pl.ds(start, count, stride=dilation)

        def one_class(r, carry):
            first = rows(0, QBLK, r)
            o, l = unit(qf[first, :], kf[first, :], vf[first, :], bias_causal[...])
            ob[idx, first, :] = o
            lb[idx, first, :] = l

            def later(blk, c):
                qr = rows(blk * QBLK, QBLK, r)
                kr = rows((blk - 1) * QBLK, 2 * QBLK, r)
                o, l = unit(qf[qr, :], kf[kr, :], vf[kr, :], bias_band[...])
                ob[idx, qr, :] = o
                lb[idx, qr, :] = l
                return c

            if n_sub_blocks > 1:
                lax.fori_loop(1, n_sub_blocks, later, 0)
            return carry

        if dilation == 1:
            one_class(0, 0)
        else:
            lax.fori_loop(0, dilation, one_class, 0)

    for idx, (_, dilation) in enumerate(DILATION_PAIRS):
        branch(idx, dilation)

    chunk = 256

    def merge(c, carry):
        r = pl.ds(pl.multiple_of(c * chunk, chunk), chunk)
        l0, l1, l2 = lb[0, r, :], lb[1, r, :], lb[2, r, :]
        m = jnp.maximum(jnp.maximum(l0, l1), l2)
        e0, e1, e2 = jnp.exp(l0 - m), jnp.exp(l1 - m), jnp.exp(l2 - m)
        o = (e0 * ob[0, r, :] + e1 * ob[1, r, :] + e2 * ob[2, r, :]) / (e0 + e1 + e2)
        o_ref[r, :] = o.astype(o_ref.dtype)
        return carry

    lax.fori_loop(0, seq // chunk, merge, 0)


def _attention(q, k, v, batch):
    t, aw = q.shape
    seq = t // batch
    view = lambda a: a.reshape(seq, batch * aw)
    blk = pl.BlockSpec((seq, LANES), lambda j: (0, j))
    slab = lambda n: pltpu.VMEM((n, seq, LANES), F32)
    out = pl.pallas_call(
        _attn_kernel,
        grid=(batch * aw // LANES,),
        in_specs=[blk, blk, blk],
        out_specs=blk,
        out_shape=jax.ShapeDtypeStruct((seq, batch * aw), BF16),
        scratch_shapes=[pltpu.VMEM((seq, LANES), F32)] * 3 + [slab(3), slab(3),
                        pltpu.VMEM((QBLK, 2 * QBLK), F32), pltpu.VMEM((QBLK, QBLK), F32)],
        compiler_params=pltpu.CompilerParams(vmem_limit_bytes=VMEM_LIMIT),
    )(view(q), view(k), view(v))
    return out.reshape(t, aw)


def _ssm_kernel(u_ref, bh_ref, ch_ref, lam_ref, d_ref, wg_ref, bg_ref, gain_ref, o_ref,
                bu, xs, st, *, batch):
    half_states = bh_ref.shape[2] // 2
    half_ch = bh_ref.shape[1]
    cchunk = 256

    @pl.when(pl.program_id(0) == 0)
    def _():
        st[...] = jnp.zeros_like(st)

    for h in range(2):
        bu[:, h * 2 * half_states:(h + 1) * 2 * half_states] = _dot(
            u_ref[:, h * half_ch:(h + 1) * half_ch], bh_ref[h])

    def step(i, carry):
        r = pl.ds(pl.multiple_of(i * batch, batch), batch)
        for h in range(2):
            for c in range(0, half_states, cchunk):
                cre = slice(h * 2 * half_states + c, h * 2 * half_states + c + cchunk)
                cim = slice(cre.start + half_states, cre.stop + half_states)
                cl = slice(h * half_states + c, h * half_states + c + cchunk)
                lr = lam_ref[0:1, cl]
                li = lam_ref[1:2, cl]
                xr = st[:, cre]
                xi = st[:, cim]
                nr = lr * xr - li * xi + bu[r, cre]
                ni = lr * xi + li * xr + bu[r, cim]
                st[:, cre] = nr
                st[:, cim] = ni
                xs[r, cre] = nr.astype(BF16)
                xs[r, cim] = ni.astype(BF16)
        return carry

    lax.fori_loop(0, SCAN_STEPS, step, 0)

    uf = u_ref[...].astype(F32)
    y = jnp.concatenate(
        [_dot(xs[:, h * 2 * half_states:(h + 1) * 2 * half_states], ch_ref[h]) for h in range(2)],
        axis=-1) + d_ref[...] * uf
    y = jax.nn.gelu(y)
    gate = _dot(y.astype(BF16), wg_ref[...]) + bg_ref[...]
    z = y * (1.0 / (1.0 + jnp.exp(-gate)))
    z = z * lax.rsqrt(jnp.mean(z * z, axis=-1, keepdims=True) + RMS_EPS) * gain_ref[...]
    o_ref[...] = z.astype(o_ref.dtype)


def _ssm(u, bh, ch, lam, d_skip, w_glu, b_glu, gain, batch):
    t, sw = u.shape
    m = SCAN_STEPS * batch
    n_state = bh.shape[0] * bh.shape[2]
    full = lambda a: pl.BlockSpec(a.shape, lambda i: (0,) * a.ndim)
    rows = pl.BlockSpec((m, sw), lambda i: (i, 0))
    consts = (bh, ch, lam, d_skip, w_glu, b_glu, gain)
    return pl.pallas_call(
        functools.partial(_ssm_kernel, batch=batch),
        grid=(t // m,),
        in_specs=[rows] + [full(a) for a in consts],
        out_specs=rows,
        out_shape=jax.ShapeDtypeStruct((t, sw), BF16),
        scratch_shapes=[pltpu.VMEM((m, n_state), F32), pltpu.VMEM((m, n_state), BF16),
                        pltpu.VMEM((batch, n_state), F32)],
        compiler_params=pltpu.CompilerParams(dimension_semantics=("arbitrary",),
                                             vmem_limit_bytes=VMEM_LIMIT),
    )(u, *consts)


def _ssm_params(a_re, a_im, log_dt, b_re, b_im, c_re, c_im):
    g, p = a_re.shape
    n = b_re.shape[-1]
    dt = jnp.exp(log_dt)[:, None]
    mag = jnp.exp(a_re * dt)
    ang = a_im * dt
    lb_re = mag * jnp.cos(ang)
    lb_im = mag * jnp.sin(ang)
    den = a_re * a_re + a_im * a_im
    nr = lb_re - 1.0
    ni = lb_im
    cr = (nr * a_re + ni * a_im) / den
    ci = (ni * a_re - nr * a_im) / den
    bb_re = cr[..., None] * b_re - ci[..., None] * b_im
    bb_im = cr[..., None] * b_im + ci[..., None] * b_re
    gh = g // 2
    eye = jnp.eye(gh, dtype=F32)

    def pack_b(bb):
        return jnp.einsum('gpn,gh->gnhp', bb, eye).reshape(gh * n, gh * p)

    def pack_c(c):
        return jnp.einsum('gnp,gh->gphn', c, eye).reshape(gh * p, gh * n)

    bh = jnp.stack([jnp.concatenate([pack_b(bb_re[s]), pack_b(bb_im[s])], axis=1)
                    for s in (slice(0, gh), slice(gh, g))])
    ch = jnp.stack([jnp.concatenate([pack_c(c_re[s]), -pack_c(c_im[s])], axis=0)
                    for s in (slice(0, gh), slice(gh, g))])
    lam = jnp.stack([lb_re.reshape(-1), lb_im.reshape(-1)])
    return bh.astype(BF16), ch.astype(BF16), lam


def _layer_norm(x, g, b):
    mu = jnp.mean(x, axis=-1, keepdims=True)
    xc = x - mu
    var = jnp.mean(xc * xc, axis=-1, keepdims=True)
    return xc * lax.rsqrt(var + LN_EPS) * g + b


def _post_kernel(h_ref, a_ref, s_ref, ag_ref, wo_ref, bo_ref, g1_ref, b1_ref,
                 w1_ref, bf1_ref, w2_ref, bf2_ref, g2_ref, b2_ref, o_ref, *, alpha):
    aw = a_ref.shape[1]
    a = a_ref[...].astype(F32)
    a = a * lax.rsqrt(jnp.mean(a * a, axis=-1, keepdims=True) + RMS_EPS) * ag_ref[...]
    mix = (_dot(a.astype(BF16), wo_ref[0:aw, :]) + _dot(s_ref[...], wo_ref[aw:, :]) + bo_ref[...])
    h1 = _layer_norm(alpha * h_ref[...] + mix, g1_ref[...], b1_ref[...])
    h1b = h1.astype(BF16)
    ff = jnp.zeros_like(h1)
    for c in range(0, w1_ref.shape[1], FF_CHUNK):
        act = jnp.maximum(_dot(h1b, w1_ref[:, c:c + FF_CHUNK]) + bf1_ref[:, c:c + FF_CHUNK], 0.0)
        ff = ff + _dot((act * act).astype(BF16), w2_ref[c:c + FF_CHUNK, :])
    o_ref[...] = _layer_norm(alpha * h1 + ff + bf2_ref[...], g2_ref[...], b2_ref[...])


def _post(h, attn, ssm, consts, alpha):
    t, d = h.shape
    tm = ROW_TILE
    rows = lambda width: pl.BlockSpec((tm, width), lambda i: (i, 0))
    full = lambda a: pl.BlockSpec(a.shape, lambda i: (0,) * a.ndim, pipeline_mode=pl.Buffered(1))
    return pl.pallas_call(
        functools.partial(_post_kernel, alpha=alpha),
        grid=(t // tm,),
        in_specs=[rows(d), rows(attn.shape[1]), rows(ssm.shape[1])] + [full(a) for a in consts],
        out_specs=rows(d),
        out_shape=jax.ShapeDtypeStruct((t, d), F32),
        compiler_params=pltpu.CompilerParams(vmem_limit_bytes=VMEM_LIMIT),
    )(h, attn, ssm, *consts)


def kernel(x, positions, w_in, attn_gain, ssm_gain, ssm_a_re, ssm_a_im, ssm_log_dt, ssm_b_re,
           ssm_b_im, ssm_c_re, ssm_c_im, ssm_d, w_glu, b_glu, w_out, b_out, ln1_g, ln1_b,
           w_ff1, b_ff1, w_ff2, b_ff2, ln2_g, ln2_b):
    batch, seq, d = x.shape
    depth = w_in.shape[0]
    aw = attn_gain.shape[1]
    t = batch * seq
    alpha = (2.0 * depth) ** 0.25
    row = lambda a: a.reshape(1, -1).astype(F32)

    half = HEAD_DIM // 2
    inv_freq = ROPE_THETA ** (-jnp.arange(half, dtype=F32) * 2.0 / HEAD_DIM)
    freq_row = jnp.tile(inv_freq, LANES // half).reshape(1, LANES)
    sign_row = jnp.tile(jnp.concatenate([-jnp.ones(half, F32), jnp.ones(half, F32)]),
                        LANES // HEAD_DIM).reshape(1, LANES)
    pos_col = positions.T.reshape(t, 1).astype(F32)
    cos_t, sin_t = _rope_tables(pos_col, freq_row, sign_row)

    h = x.transpose(1, 0, 2).reshape(t, d)
    for l in range(depth):
        q, k, v, u = _proj(h, w_in[l].astype(BF16), cos_t, sin_t, aw)
        attn = _attention(q, k, v, batch)
        bh, ch, lam = _ssm_params(ssm_a_re[l], ssm_a_im[l], ssm_log_dt[l], ssm_b_re[l],
                                  ssm_b_im[l], ssm_c_re[l], ssm_c_im[l])
        ssm = _ssm(u, bh, ch, lam, row(ssm_d[l]), w_glu[l].astype(BF16), row(b_glu[l]),
                   row(ssm_gain[l]), batch)
        consts = (row(attn_gain[l]), w_out[l].astype(BF16), row(b_out[l]), row(ln1_g[l]),
                  row(ln1_b[l]), w_ff1[l].astype(BF16), row(b_ff1[l]), w_ff2[l].astype(BF16),
                  row(b_ff2[l]), row(ln2_g[l]), row(ln2_b[l]))
        h = _post(h, attn, ssm, consts, alpha)
    return h.reshape(seq, batch, d).transpose(1, 0, 2)
```

```python
import functools
import math

import jax
import jax.numpy as jnp
from jax import lax
from jax.experimental import pallas as pl
from jax.experimental.pallas import tpu as pltpu

F32 = jnp.float32
BF16 = jnp.bfloat16

HEAD_DIM = 64
LANES = 128
DILATION_PAIRS = ((128, 1), (512, 4), (2048, 16))
QBLK = 128
N_DIL4 = 4
N_DIL16 = 16
GROUP = 8
SLOTS = 2 * GROUP
ROPE_THETA = 10000.0
LN_EPS = 1e-5
RMS_EPS = 1e-6
NEG_INF = -1e30
LOG2E = math.log2(math.e)
VMEM_LIMIT = 56 * 1024 * 1024

TBLK = 16
FF_CHUNK = 1024


def _dot(a, b):
    return jnp.dot(a, b, preferred_element_type=F32)


def _dot_nt(a, b):
    return lax.dot_general(a, b, (((1,), (1,)), ((), ())), preferred_element_type=F32)


def _layer_spec(arr, l, single_buffer=False):
    mode = dict(pipeline_mode=pl.Buffered(1)) if single_buffer else {}
    return pl.BlockSpec((None,) + arr.shape[1:], lambda i: (l,) + (0,) * (arr.ndim - 1), **mode)


def _rope_table_kernel(pos_ref, freq_ref, sign_ref, cos_ref, sin_ref):
    ang = pos_ref[...] * freq_ref[...]
    cos_ref[...] = jnp.cos(ang)
    sin_ref[...] = jnp.sin(ang) * sign_ref[...]


def _rope_tables(pos_col, freq_row, sign_row):
    t = pos_col.shape[0]
    tm = 2048
    row = pl.BlockSpec((1, LANES), lambda i: (0, 0))
    tab = pl.BlockSpec((tm, LANES), lambda i: (i, 0))
    return pl.pallas_call(
        _rope_table_kernel,
        grid=(t // tm,),
        in_specs=[pl.BlockSpec((tm, 1), lambda i: (i, 0)), row, row],
        out_specs=[tab, tab],
        out_shape=[jax.ShapeDtypeStruct((t, LANES), F32)] * 2,
    )(pos_col, freq_row, sign_row)


def _proj_kernel(h_ref, w_ref, cos_ref, sin_ref, q_ref, k_ref, v_ref, u_ref, *, aw):
    nb, tb, d = h_ref.shape
    rows = nb * tb
    hb = h_ref[...].reshape(rows, d).astype(BF16)
    cos = cos_ref[...]
    sin = sin_ref[...]
    lane = lax.broadcasted_iota(jnp.int32, (1, LANES), 1)
    first_half = (lane % HEAD_DIM) < (HEAD_DIM // 2)

    def rope(x):
        rot = jnp.where(first_half,
                        pltpu.roll(x, LANES - HEAD_DIM // 2, 1),
                        pltpu.roll(x, HEAD_DIM // 2, 1))
        return x * cos + rot * sin

    def put(ref, sl, val):
        ref[:, :, sl] = val.astype(BF16).reshape(nb, tb, val.shape[-1])

    q = _dot(hb, w_ref[:, 0:aw])
    k = _dot(hb, w_ref[:, aw:2 * aw])
    scale = HEAD_DIM ** -0.5 * LOG2E
    for j in range(aw // LANES):
        sl = slice(j * LANES, (j + 1) * LANES)
        put(q_ref, sl, rope(q[:, sl]) * scale)
        put(k_ref, sl, rope(k[:, sl]))
    put(v_ref, slice(None), _dot(hb, w_ref[:, 2 * aw:3 * aw]))
    put(u_ref, slice(None), _dot(hb, w_ref[:, 3 * aw:]))


def _proj(h, h_spec, w_in, l, cos_t, sin_t, aw, batch, n_tblk):
    d, n = w_in.shape[1:]
    sw = n - 3 * aw
    rows = batch * TBLK
    tab = pl.BlockSpec((rows, LANES), lambda i: (i, 0))
    out = lambda width: pl.BlockSpec((None, batch, TBLK, width), lambda i: (i, 0, 0, 0))
    shape = lambda width: jax.ShapeDtypeStruct((n_tblk, batch, TBLK, width), BF16)
    return pl.pallas_call(
        functools.partial(_proj_kernel, aw=aw),
        grid=(n_tblk,),
        in_specs=[h_spec, _layer_spec(w_in, l), tab, tab],
        out_specs=[out(aw), out(aw), out(aw), out(sw)],
        out_shape=[shape(aw), shape(aw), shape(aw), shape(sw)],
        compiler_params=pltpu.CompilerParams(vmem_limit_bytes=VMEM_LIMIT),
    )(h, w_in, cos_t, sin_t)


def _attn_kernel(q_ref, k_ref, v_ref, o_ref,
                 xf, x4, q12, k12, v12, q3, k3, v3, acc12, den12, mx12, acc3, den3, mx3, onat,
                 bias2, bias_c, s_all, mc_all, p_all):
    seq = q_ref.shape[0] * q_ref.shape[1]
    n4 = seq // N_DIL4
    pad_blk = QBLK + n4
    base2 = QBLK + seq

    head0 = lax.broadcasted_iota(jnp.int32, (1, LANES), 1) < HEAD_DIM
    ones_h0 = jnp.where(head0, 1.0, 0.0).astype(BF16)
    ones_h1 = jnp.where(head0, 0.0, 1.0).astype(BF16)

    qi = lax.broadcasted_iota(jnp.int32, (QBLK, 2 * QBLK), 0)
    kj = lax.broadcasted_iota(jnp.int32, (QBLK, 2 * QBLK), 1)
    band = (kj >= qi) & (kj <= qi + QBLK)
    bias2[0] = jnp.where(band, 0.0, NEG_INF).astype(F32)
    bias2[1] = jnp.where(band & (kj >= QBLK), 0.0, NEG_INF).astype(F32)
    qi = lax.broadcasted_iota(jnp.int32, (QBLK, QBLK), 0)
    kj = lax.broadcasted_iota(jnp.int32, (QBLK, QBLK), 1)
    bias_c[...] = jnp.where(kj <= qi, 0.0, NEG_INF).astype(F32)

    nat = [ref[...].reshape(seq, LANES) for ref in (q_ref, k_ref, v_ref)]
    for i in range(3):
        xf[i] = nat[i].astype(F32)
    for r in range(N_DIL4):
        for i in range(3):
            x4[i, r * n4:(r + 1) * n4, :] = xf[i, pl.ds(r, n4, stride=N_DIL4), :]
    q12[0:seq, :] = nat[0]
    zero_blk = jnp.zeros((QBLK, LANES), BF16)
    for dst, i in ((k12, 1), (v12, 2)):
        dst[0:QBLK, :] = zero_blk
        dst[QBLK:base2, :] = nat[i]
        for r in range(N_DIL4):
            b = base2 + r * pad_blk
            dst[b:b + QBLK, :] = zero_blk
            dst[b + QBLK:b + pad_blk, :] = x4[i, r * n4:(r + 1) * n4, :].astype(BF16)
    for r in range(N_DIL4):
        q12[seq + r * n4:seq + (r + 1) * n4, :] = x4[0, r * n4:(r + 1) * n4, :].astype(BF16)
    for r16 in range(N_DIL16):
        rows = pl.ds((r16 % N_DIL4) * n4 + r16 // N_DIL4, QBLK, stride=N_DIL4)
        for dst, i in ((q3, 0), (k3, 1), (v3, 2)):
            dst[r16 * QBLK:(r16 + 1) * QBLK, :] = x4[i, rows, :].astype(BF16)

    def stage_a(s_buf, nk, qb, kb):
        zero = jnp.zeros_like(kb)
        kblk = jnp.concatenate([jnp.where(head0, kb, zero), jnp.where(head0, zero, kb)], axis=0)
        s_buf[:, 0:2 * nk] = _dot_nt(qb, kblk)

    def stage_b(s_buf, mc_buf, p_buf, nk, bias):
        ms = []
        for a in range(2):
            sa = s_buf[:, a * nk:(a + 1) * nk] + bias
            m = jnp.max(sa, axis=-1, keepdims=True)
            p_buf[:, a * nk:(a + 1) * nk] = jnp.exp2(sa - m).astype(BF16)
            ms.append(jnp.broadcast_to(m, (QBLK, LANES)))
        mc_buf[...] = jnp.where(head0, ms[0], ms[1])

    def stage_c(p_buf, nk, vb):
        zero = jnp.zeros_like(vb)
        w = jnp.concatenate(
            [jnp.concatenate([jnp.where(head0, vb, zero), jnp.broadcast_to(ones_h0, vb.shape)], axis=1),
             jnp.concatenate([jnp.where(head0, zero, vb), jnp.broadcast_to(ones_h1, vb.shape)], axis=1)],
            axis=0)
        r = _dot(p_buf[:, 0:2 * nk], w)
        return r[:, :LANES], r[:, LANES:]

    def run_pipeline(n_units, a_fn, bias_fn, c_fn, nk):
        def a(u):
            u = jnp.int32(u)
            a_fn(u, s_all.at[u % SLOTS])

        def b(u):
            u = jnp.int32(u)
            slot = u % SLOTS
            stage_b(s_all.at[slot], mc_all.at[slot], p_all.at[slot], nk, bias_fn(u))

        def c(u):
            u = jnp.int32(u)
            slot = u % SLOTS
            c_fn(u, p_all.at[slot], mc_all.at[slot])

        g = GROUP
        for u in range(2 * g):
            a(u)
        for u in range(g):
            b(u)

        def body(j, carry):
            u0 = g * j
            for i in range(g):
                c(u0 + i)
            for i in range(g):
                b(u0 + g + i)
            for i in range(g):
                a(u0 + 2 * g + i)
            return carry

        lax.fori_loop(0, n_units // g - 2, body, 0)
        for u in range(n_units - 2 * g, n_units - g):
            c(u)
        for u in range(n_units - g, n_units):
            b(u)
        for u in range(n_units - g, n_units):
            c(u)

    blocks1 = seq // QBLK
    blocks4 = n4 // QBLK

    def offsets12(u):
        v = u - blocks1
        r = v // blocks4
        blk = v % blocks4
        is_nat = u < blocks1
        q_off = jnp.where(is_nat, u * QBLK, seq + v * QBLK)
        k_off = jnp.where(is_nat, u * QBLK, base2 + r * pad_blk + blk * QBLK)
        first = jnp.where(is_nat, u == 0, blk == 0)
        return pl.multiple_of(q_off, QBLK), pl.multiple_of(k_off, QBLK), first.astype(jnp.int32)

    def a12(u, s_buf):
        q_off, k_off, _ = offsets12(u)
        stage_a(s_buf, 2 * QBLK, q12[pl.ds(q_off, QBLK), :], k12[pl.ds(k_off, 2 * QBLK), :])

    def bias12(u):
        return bias2[offsets12(u)[2]]

    def c12(u, p_buf, mc_buf):
        q_off, k_off, _ = offsets12(u)
        acc, den = stage_c(p_buf, 2 * QBLK, v12[pl.ds(k_off, 2 * QBLK), :])
        acc12[pl.ds(q_off, QBLK), :] = acc
        den12[pl.ds(q_off, QBLK), :] = den
        mx12[pl.ds(q_off, QBLK), :] = mc_buf[...]

    run_pipeline(blocks1 + N_DIL4 * blocks4, a12, bias12, c12, 2 * QBLK)

    def rows3(u):
        return pl.ds(pl.multiple_of(u * QBLK, QBLK), QBLK)

    def a3(u, s_buf):
        stage_a(s_buf, QBLK, q3[rows3(u), :], k3[rows3(u), :])

    def c3(u, p_buf, mc_buf):
        acc, den = stage_c(p_buf, QBLK, v3[rows3(u), :])
        dst = pl.ds((u % N_DIL4) * n4 + u // N_DIL4, QBLK, stride=N_DIL4)
        acc3[dst, :] = acc
        den3[dst, :] = den
        mx3[dst, :] = mc_buf[...]

    run_pipeline(N_DIL16, a3, lambda u: bias_c[...], c3, QBLK)

    def merge(idx, carry):
        r = idx // blocks4
        c = idx % blocks4
        rn = pl.ds(c * QBLK * N_DIL4 + r, QBLK, stride=N_DIL4)
        r3 = pl.ds(pl.multiple_of(r * n4 + c * QBLK, QBLK), QBLK)
        r2 = pl.ds(pl.multiple_of(seq + r * n4 + c * QBLK, QBLK), QBLK)
        ma, mb, mc = mx12[rn, :], mx12[r2, :], mx3[r3, :]
        m = jnp.maximum(jnp.maximum(ma, mb), mc)
        wa, wb, wc = jnp.exp2(ma - m), jnp.exp2(mb - m), jnp.exp2(mc - m)
        num = wa * acc12[rn, :] + wb * acc12[r2, :] + wc * acc3[r3, :]
        den = wa * den12[rn, :] + wb * den12[r2, :] + wc * den3[r3, :]
        onat[rn, :] = num / den
        return carry

    lax.fori_loop(0, N_DIL4 * blocks4, merge, 0)
    o_ref[...] = onat[...].astype(o_ref.dtype).reshape(o_ref.shape)


def _attention(q, k, v):
    n_tblk, batch, tb, aw = q.shape
    seq = n_tblk * tb
    assert seq == N_DIL16 * QBLK and DILATION_PAIRS == ((128, 1), (512, 4), (2048, 16))
    pairs = aw // LANES
    blk = pl.BlockSpec((n_tblk, None, tb, LANES), lambda j: (0, j // pairs, 0, j % pairs))
    n4 = seq // N_DIL4
    pad_rows = QBLK + seq + N_DIL4 * (QBLK + n4)
    vm = pltpu.VMEM
    scratch = [
        vm((3, seq, LANES), F32), vm((3, seq, LANES), F32),
        vm((2 * seq, LANES), BF16), vm((pad_rows, LANES), BF16), vm((pad_rows, LANES), BF16),
        vm((seq, LANES), BF16), vm((seq, LANES), BF16), vm((seq, LANES), BF16),
        vm((2 * seq, LANES), F32), vm((2 * seq, LANES), F32), vm((2 * seq, LANES), F32),
        vm((seq, LANES), F32), vm((seq, LANES), F32), vm((seq, LANES), F32),
        vm((seq, LANES), F32),
        vm((2, QBLK, 2 * QBLK), F32), vm((QBLK, QBLK), F32),
        vm((SLOTS, QBLK, 4 * QBLK), F32), vm((SLOTS, QBLK, LANES), F32),
        vm((SLOTS, QBLK, 4 * QBLK), BF16),
    ]
    return pl.pallas_call(
        _attn_kernel,
        grid=(batch * pairs,),
        in_specs=[blk, blk, blk],
        out_specs=blk,
        out_shape=jax.ShapeDtypeStruct(q.shape, BF16),
        scratch_shapes=scratch,
        compiler_params=pltpu.CompilerParams(vmem_limit_bytes=VMEM_LIMIT),
    )(q, k, v)


def _ssm_kernel(u_ref, perm_ref, permt_ref, bh_ref, ch_ref, lam_ref, d_ref, wg_ref, bg_ref,
                gain_ref, o_ref, up, bu, xs, st):
    batch, tb, sw = u_ref.shape
    half_states = bh_ref.shape[2] // 2
    half_ch = bh_ref.shape[1]
    cchunk = 256

    @pl.when(pl.program_id(0) == 0)
    def _():
        st[...] = jnp.zeros_like(st)

    up[...] = _dot(perm_ref[...], u_ref[...].reshape(batch * tb, sw)).astype(BF16)

    for h in range(2):
        bu[:, h * 2 * half_states:(h + 1) * 2 * half_states] = _dot(
            up[:, h * half_ch:(h + 1) * half_ch], bh_ref[h])

    def step(i, carry):
        r = pl.ds(pl.multiple_of(i * batch, batch), batch)
        for h in range(2):
            for c in range(0, half_states, cchunk):
                cre = slice(h * 2 * half_states + c, h * 2 * half_states + c + cchunk)
                cim = slice(cre.start + half_states, cre.stop + half_states)
                cl = slice(h * half_states + c, h * half_states + c + cchunk)
                lr = lam_ref[0:1, cl]
                li = lam_ref[1:2, cl]
                xr = st[:, cre]
                xi = st[:, cim]
                nr = lr * xr - li * xi + bu[r, cre]
                ni = lr * xi + li * xr + bu[r, cim]
                st[:, cre] = nr
                st[:, cim] = ni
                xs[r, cre] = nr.astype(BF16)
                xs[r, cim] = ni.astype(BF16)
        return carry

    lax.fori_loop(0, tb, step, 0)

    y = jnp.concatenate(
        [_dot(xs[:, h * 2 * half_states:(h + 1) * 2 * half_states], ch_ref[h]) for h in range(2)],
        axis=-1) + d_ref[...] * up[...].astype(F32)
    y = jax.nn.gelu(y)
    gate = _dot(y.astype(BF16), wg_ref[...]) + bg_ref[...]
    z = y * (1.0 / (1.0 + jnp.exp(-gate)))
    z = z * lax.rsqrt(jnp.mean(z * z, axis=-1, keepdims=True) + RMS_EPS) * gain_ref[...]
    o_ref[...] = _dot(permt_ref[...], z.astype(BF16)).astype(o_ref.dtype)


def _ssm(u, perm, perm_t, l, bh, ch, lam, d_skip, w_glu, b_glu, gain):
    n_tblk, batch, tb, sw = u.shape
    m = batch * tb
    n_state = bh.shape[1] * bh.shape[3]
    full = lambda a: pl.BlockSpec(a.shape, lambda i: (0,) * a.ndim)
    consts = (bh, ch, lam, d_skip, w_glu, b_glu, gain)
    return pl.pallas_call(
        _ssm_kernel,
        grid=(n_tblk,),
        in_specs=[pl.BlockSpec((None, batch, tb, sw), lambda i: (i, 0, 0, 0)), full(perm),
                  full(perm_t)] + [_layer_spec(a, l) for a in consts],
        out_specs=pl.BlockSpec((m, sw), lambda i: (i, 0)),
        out_shape=jax.ShapeDtypeStruct((n_tblk * m, sw), BF16),
        scratch_shapes=[pltpu.VMEM((m, sw), BF16), pltpu.VMEM((m, n_state), F32),
                        pltpu.VMEM((m, n_state), BF16), pltpu.VMEM((batch, n_state), F32)],
        compiler_params=pltpu.CompilerParams(dimension_semantics=("arbitrary",),
                                             vmem_limit_bytes=VMEM_LIMIT),
    )(u, perm, perm_t, *consts)


def _ssm_params(a_re, a_im, log_dt, b_re, b_im, c_re, c_im):
    depth, g, p = a_re.shape
    n = b_re.shape[-1]
    dt = jnp.exp(log_dt)[..., None]
    mag = jnp.exp(a_re * dt)
    ang = a_im * dt
    lb_re = mag * jnp.cos(ang)
    lb_im = mag * jnp.sin(ang)
    den = a_re * a_re + a_im * a_im
    nr = lb_re - 1.0
    ni = lb_im
    cr = (nr * a_re + ni * a_im) / den
    ci = (ni * a_re - nr * a_im) / den
    bb_re = cr[..., None] * b_re - ci[..., None] * b_im
    bb_im = cr[..., None] * b_im + ci[..., None] * b_re
    gh = g // 2
    eye = jnp.eye(gh, dtype=F32)
    halves = (slice(0, gh), slice(gh, g))

    def pack_b(bb):
        return jnp.einsum('lgpn,gh->lgnhp', bb, eye).reshape(depth, gh * n, gh * p)

    def pack_c(c):
        return jnp.einsum('lgnp,gh->lgphn', c, eye).reshape(depth, gh * p, gh * n)

    bh = jnp.stack([jnp.concatenate([pack_b(bb_re[:, s]), pack_b(bb_im[:, s])], axis=2)
                    for s in halves], axis=1)
    ch = jnp.stack([jnp.concatenate([pack_c(c_re[:, s]), -pack_c(c_im[:, s])], axis=1)
                    for s in halves], axis=1)
    lam = jnp.stack([lb_re.reshape(depth, -1), lb_im.reshape(depth, -1)], axis=1)
    return bh.astype(BF16), ch.astype(BF16), lam


def _layer_norm(x, g, b):
    mu = jnp.mean(x, axis=-1, keepdims=True)
    xc = x - mu
    var = jnp.mean(xc * xc, axis=-1, keepdims=True)
    return xc * lax.rsqrt(var + LN_EPS) * g + b


def _post_kernel(h_ref, a_ref, s_ref, ag_ref, wo_ref, bo_ref, g1_ref, b1_ref,
                 w1_ref, bf1_ref, w2_ref, bf2_ref, g2_ref, b2_ref, o_ref, *, alpha):
    nb, tb, d = h_ref.shape
    rows = nb * tb
    aw = a_ref.shape[-1]
    a = a_ref[...].reshape(rows, aw).astype(F32)
    a = a * lax.rsqrt(jnp.mean(a * a, axis=-1, keepdims=True) + RMS_EPS) * ag_ref[...]
    mix = (_dot(a.astype(BF16), wo_ref[0:aw, :]) + _dot(s_ref[...], wo_ref[aw:, :]) + bo_ref[...])
    h1 = _layer_norm(alpha * h_ref[...].reshape(rows, d) + mix, g1_ref[...], b1_ref[...])
    h1b = h1.astype(BF16)
    ff = jnp.zeros_like(h1)
    for c in range(0, w1_ref.shape[1], FF_CHUNK):
        act = jnp.maximum(_dot(h1b, w1_ref[:, c:c + FF_CHUNK]) + bf1_ref[:, c:c + FF_CHUNK], 0.0)
        ff = ff + _dot((act * act).astype(BF16), w2_ref[c:c + FF_CHUNK, :])
    out = _layer_norm(alpha * h1 + ff + bf2_ref[...], g2_ref[...], b2_ref[...])
    o_ref[...] = out.reshape(nb, tb, d)


def _post(h, h_spec, attn, ssm, l, consts, alpha, out_shape, out_spec):
    n_tblk, batch, tb, aw = attn.shape
    rows = batch * tb
    return pl.pallas_call(
        functools.partial(_post_kernel, alpha=alpha),
        grid=(n_tblk,),
        in_specs=[h_spec, pl.BlockSpec((None, batch, tb, aw), lambda i: (i, 0, 0, 0)),
                  pl.BlockSpec((rows, ssm.shape[1]), lambda i: (i, 0))]
                 + [_layer_spec(a, l, single_buffer=True) for a in consts],
        out_specs=out_spec,
        out_shape=out_shape,
        compiler_params=pltpu.CompilerParams(vmem_limit_bytes=VMEM_LIMIT),
    )(h, attn, ssm, *consts)


def kernel(x, positions, w_in, attn_gain, ssm_gain, ssm_a_re, ssm_a_im, ssm_log_dt, ssm_b_re,
           ssm_b_im, ssm_c_re, ssm_c_im, ssm_d, w_glu, b_glu, w_out, b_out, ln1_g, ln1_b,
           w_ff1, b_ff1, w_ff2, b_ff2, ln2_g, ln2_b):
    batch, seq, d = x.shape
    depth = w_in.shape[0]
    aw = attn_gain.shape[1]
    t = batch * seq
    n_tblk = seq // TBLK
    rows = batch * TBLK
    alpha = (2.0 * depth) ** 0.25
    vec = lambda a: a.reshape(depth, 1, -1).astype(F32)

    half = HEAD_DIM // 2
    inv_freq = ROPE_THETA ** (-jnp.arange(half, dtype=F32) * 2.0 / HEAD_DIM)
    freq_row = jnp.tile(inv_freq, LANES // half).reshape(1, LANES)
    sign_row = jnp.tile(jnp.concatenate([-jnp.ones(half, F32), jnp.ones(half, F32)]),
                        LANES // HEAD_DIM).reshape(1, LANES)
    pos_col = positions.reshape(batch, n_tblk, TBLK).transpose(1, 0, 2).reshape(t, 1).astype(F32)
    cos_t, sin_t = _rope_tables(pos_col, freq_row, sign_row)

    r_out = jnp.arange(rows, dtype=jnp.int32)
    r_in = (r_out % batch) * TBLK + r_out // batch
    perm = (r_in[:, None] == jnp.arange(rows, dtype=jnp.int32)[None, :]).astype(BF16)
    perm_t = perm.T

    w_in_b = w_in.astype(BF16)
    bh, ch, lam = _ssm_params(ssm_a_re, ssm_a_im, ssm_log_dt, ssm_b_re, ssm_b_im, ssm_c_re, ssm_c_im)
    ssm_consts = (bh, ch, lam, vec(ssm_d), w_glu.astype(BF16), vec(b_glu), vec(ssm_gain))
    post_consts = (vec(attn_gain), w_out.astype(BF16), vec(b_out), vec(ln1_g), vec(ln1_b),
                   w_ff1.astype(BF16), vec(b_ff1), w_ff2.astype(BF16), vec(b_ff2), vec(ln2_g),
                   vec(ln2_b))

    by_seq_spec = pl.BlockSpec((batch, None, TBLK, d), lambda i: (0, i, 0, 0))
    by_blk_spec = pl.BlockSpec((batch, TBLK, d), lambda i: (i, 0, 0))
    h = x.reshape(batch, n_tblk, TBLK, d)
    h_spec = by_seq_spec
    for l in range(depth):
        q, k, v, u = _proj(h, h_spec, w_in_b, l, cos_t, sin_t, aw, batch, n_tblk)
        attn = _attention(q, k, v)
        ssm = _ssm(u, perm, perm_t, l, *ssm_consts)
        if l == depth - 1:
            out_shape, out_spec = jax.ShapeDtypeStruct((batch, n_tblk, TBLK, d), F32), by_seq_spec
        else:
            out_shape, out_spec = jax.ShapeDtypeStruct((n_tblk * batch, TBLK, d), F32), by_blk_spec
        h = _post(h, h_spec, attn, ssm, l, post_consts, alpha, out_shape, out_spec)
        h_spec = by_blk_spec
    return h.reshape(batch, seq, d)
```

```python
import functools
import math

import jax
import jax.numpy as jnp
from jax import lax
from jax.experimental import pallas as pl
from jax.experimental.pallas import tpu as pltpu

F32 = jnp.float32
BF16 = jnp.bfloat16

HEAD_DIM = 64
LANES = 128
DILATION_PAIRS = ((128, 1), (512, 4), (2048, 16))
QBLK = 128
N_DIL4 = 4
N_DIL16 = 16
GROUP = 8
SLOTS = 2 * GROUP
ROPE_THETA = 10000.0
LN_EPS = 1e-5
RMS_EPS = 1e-6
NEG_INF = -1e30
LOG2E = math.log2(math.e)
VMEM_LIMIT = 56 * 1024 * 1024

TBLK = 16
FF_CHUNK = 1024


def _dot(a, b):
    return jnp.dot(a, b, preferred_element_type=F32)


def _dot_nt(a, b):
    return lax.dot_general(a, b, (((1,), (1,)), ((), ())), preferred_element_type=F32)


def _layer_spec(arr, l, single_buffer=False):
    mode = dict(pipeline_mode=pl.Buffered(1)) if single_buffer else {}
    return pl.BlockSpec((None,) + arr.shape[1:], lambda i: (l,) + (0,) * (arr.ndim - 1), **mode)


def _rope_table_kernel(pos_ref, freq_ref, sign_ref, cos_ref, sin_ref):
    ang = pos_ref[...] * freq_ref[...]
    cos_ref[...] = jnp.cos(ang)
    sin_ref[...] = jnp.sin(ang) * sign_ref[...]


def _rope_tables(pos_col, freq_row, sign_row):
    t = pos_col.shape[0]
    tm = 2048
    row = pl.BlockSpec((1, LANES), lambda i: (0, 0))
    tab = pl.BlockSpec((tm, LANES), lambda i: (i, 0))
    return pl.pallas_call(
        _rope_table_kernel,
        grid=(t // tm,),
        in_specs=[pl.BlockSpec((tm, 1), lambda i: (i, 0)), row, row],
        out_specs=[tab, tab],
        out_shape=[jax.ShapeDtypeStruct((t, LANES), F32)] * 2,
    )(pos_col, freq_row, sign_row)


def _proj_kernel(h_ref, w_ref, cos_ref, sin_ref, q_ref, k_ref, v_ref, u_ref, *, aw):
    nb, tb, d = h_ref.shape
    rows = nb * tb
    hb = h_ref[...].reshape(rows, d).astype(BF16)
    cos = cos_ref[...]
    sin = sin_ref[...]
    lane = lax.broadcasted_iota(jnp.int32, (1, LANES), 1)
    first_half = (lane % HEAD_DIM) < (HEAD_DIM // 2)

    def rope(x):
        rot = jnp.where(first_half,
                        pltpu.roll(x, LANES - HEAD_DIM // 2, 1),
                        pltpu.roll(x, HEAD_DIM // 2, 1))
        return x * cos + rot * sin

    def put(ref, j, val):
        ref[:, j] = val.astype(BF16).reshape(nb, tb, LANES)

    q = _dot(hb, w_ref[:, 0:aw])
    k = _dot(hb, w_ref[:, aw:2 * aw])
    v = _dot(hb, w_ref[:, 2 * aw:3 * aw])
    scale = HEAD_DIM ** -0.5 * LOG2E
    for j in range(aw // LANES):
        sl = slice(j * LANES, (j + 1) * LANES)
        put(q_ref, j, rope(q[:, sl]) * scale)
        put(k_ref, j, rope(k[:, sl]))
        put(v_ref, j, v[:, sl])
    u_ref[...] = _dot(hb, w_ref[:, 3 * aw:]).astype(BF16).reshape(u_ref.shape)


def _proj(h, h_spec, w_in, l, cos_t, sin_t, aw, batch, n_tblk):
    d, n = w_in.shape[1:]
    sw = n - 3 * aw
    rows = batch * TBLK
    tab = pl.BlockSpec((rows, LANES), lambda i: (i, 0))
    pairs = aw // LANES
    qkv_spec = pl.BlockSpec((batch, pairs, TBLK, LANES), lambda i: (0, 0, i, 0))
    qkv_shape = jax.ShapeDtypeStruct((batch, pairs, n_tblk * TBLK, LANES), BF16)
    return pl.pallas_call(
        functools.partial(_proj_kernel, aw=aw),
        grid=(n_tblk,),
        in_specs=[h_spec, _layer_spec(w_in, l), tab, tab],
        out_specs=[qkv_spec] * 3 + [pl.BlockSpec((None, batch, TBLK, sw), lambda i: (i, 0, 0, 0))],
        out_shape=[qkv_shape] * 3 + [jax.ShapeDtypeStruct((n_tblk, batch, TBLK, sw), BF16)],
        compiler_params=pltpu.CompilerParams(vmem_limit_bytes=VMEM_LIMIT),
    )(h, w_in, cos_t, sin_t)


def _attn_kernel(q_ref, k_ref, v_ref, o_ref,
                 xf, x4, q12, k12, v12, q3, k3, v3, acc12, den12, mx12, acc3, den3, mx3, onat,
                 bias2, bias_c, s_all, mc_all, p_all):
    seq = q_ref.shape[0]
    n4 = seq // N_DIL4
    pad_blk = QBLK + n4
    base2 = QBLK + seq

    head0 = lax.broadcasted_iota(jnp.int32, (1, LANES), 1) < HEAD_DIM
    ones_h0 = jnp.where(head0, 1.0, 0.0).astype(BF16)
    ones_h1 = jnp.where(head0, 0.0, 1.0).astype(BF16)

    qi = lax.broadcasted_iota(jnp.int32, (QBLK, 2 * QBLK), 0)
    kj = lax.broadcasted_iota(jnp.int32, (QBLK, 2 * QBLK), 1)
    band = (kj >= qi) & (kj <= qi + QBLK)
    bias2[0] = jnp.where(band, 0.0, NEG_INF).astype(F32)
    bias2[1] = jnp.where(band & (kj >= QBLK), 0.0, NEG_INF).astype(F32)
    qi = lax.broadcasted_iota(jnp.int32, (QBLK, QBLK), 0)
    kj = lax.broadcasted_iota(jnp.int32, (QBLK, QBLK), 1)
    bias_c[...] = jnp.where(kj <= qi, 0.0, NEG_INF).astype(F32)

    nat = [ref[...] for ref in (q_ref, k_ref, v_ref)]
    for i in range(3):
        xf[i] = nat[i].astype(F32)
    for r in range(N_DIL4):
        for i in range(3):
            x4[i, r * n4:(r + 1) * n4, :] = xf[i, pl.ds(r, n4, stride=N_DIL4), :]
    q12[0:seq, :] = nat[0]
    zero_blk = jnp.zeros((QBLK, LANES), BF16)
    for dst, i in ((k12, 1), (v12, 2)):
        dst[0:QBLK, :] = zero_blk
        dst[QBLK:base2, :] = nat[i]
        for r in range(N_DIL4):
            b = base2 + r * pad_blk
            dst[b:b + QBLK, :] = zero_blk
            dst[b + QBLK:b + pad_blk, :] = x4[i, r * n4:(r + 1) * n4, :].astype(BF16)
    for r in range(N_DIL4):
        q12[seq + r * n4:seq + (r + 1) * n4, :] = x4[0, r * n4:(r + 1) * n4, :].astype(BF16)
    for r16 in range(N_DIL16):
        rows = pl.ds((r16 % N_DIL4) * n4 + r16 // N_DIL4, QBLK, stride=N_DIL4)
        for dst, i in ((q3, 0), (k3, 1), (v3, 2)):
            dst[r16 * QBLK:(r16 + 1) * QBLK, :] = x4[i, rows, :].astype(BF16)

    def stage_a(s_buf, nk, qb, kb):
        zero = jnp.zeros_like(kb)
        kblk = jnp.concatenate([jnp.where(head0, kb, zero), jnp.where(head0, zero, kb)], axis=0)
        s_buf[:, 0:2 * nk] = _dot_nt(qb, kblk)

    def stage_b(s_buf, mc_buf, p_buf, nk, bias):
        ms = []
        for a in range(2):
            sa = s_buf[:, a * nk:(a + 1) * nk] + bias
            m = jnp.max(sa, axis=-1, keepdims=True)
            p_buf[:, a * nk:(a + 1) * nk] = jnp.exp2(sa - m).astype(BF16)
            ms.append(jnp.broadcast_to(m, (QBLK, LANES)))
        mc_buf[...] = jnp.where(head0, ms[0], ms[1])

    def stage_c(p_buf, nk, vb):
        zero = jnp.zeros_like(vb)
        w = jnp.concatenate(
            [jnp.concatenate([jnp.where(head0, vb, zero), jnp.broadcast_to(ones_h0, vb.shape)], axis=1),
             jnp.concatenate([jnp.where(head0, zero, vb), jnp.broadcast_to(ones_h1, vb.shape)], axis=1)],
            axis=0)
        r = _dot(p_buf[:, 0:2 * nk], w)
        return r[:, :LANES], r[:, LANES:]

    def run_pipeline(n_units, a_fn, bias_fn, c_fn, nk):
        def a(u):
            u = jnp.int32(u)
            a_fn(u, s_all.at[u % SLOTS])

        def b(u):
            u = jnp.int32(u)
            slot = u % SLOTS
            stage_b(s_all.at[slot], mc_all.at[slot], p_all.at[slot], nk, bias_fn(u))

        def c(u):
            u = jnp.int32(u)
            slot = u % SLOTS
            c_fn(u, p_all.at[slot], mc_all.at[slot])

        g = GROUP
        for u in range(2 * g):
            a(u)
        for u in range(g):
            b(u)

        def body(j, carry):
            u0 = g * j
            for i in range(g):
                c(u0 + i)
            for i in range(g):
                b(u0 + g + i)
            for i in range(g):
                a(u0 + 2 * g + i)
            return carry

        lax.fori_loop(0, n_units // g - 2, body, 0)
        for u in range(n_units - 2 * g, n_units - g):
            c(u)
        for u in range(n_units - g, n_units):
            b(u)
        for u in range(n_units - g, n_units):
            c(u)

    blocks1 = seq // QBLK
    blocks4 = n4 // QBLK

    def offsets12(u):
        v = u - blocks1
        r = v // blocks4
        blk = v % blocks4
        is_nat = u < blocks1
        q_off = jnp.where(is_nat, u * QBLK, seq + v * QBLK)
        k_off = jnp.where(is_nat, u * QBLK, base2 + r * pad_blk + blk * QBLK)
        first = jnp.where(is_nat, u == 0, blk == 0)
        return pl.multiple_of(q_off, QBLK), pl.multiple_of(k_off, QBLK), first.astype(jnp.int32)

    def a12(u, s_buf):
        q_off, k_off, _ = offsets12(u)
        stage_a(s_buf, 2 * QBLK, q12[pl.ds(q_off, QBLK), :], k12[pl.ds(k_off, 2 * QBLK), :])

    def bias12(u):
        return bias2[offsets12(u)[2]]

    def c12(u, p_buf, mc_buf):
        q_off, k_off, _ = offsets12(u)
        acc, den = stage_c(p_buf, 2 * QBLK, v12[pl.ds(k_off, 2 * QBLK), :])
        acc12[pl.ds(q_off, QBLK), :] = acc
        den12[pl.ds(q_off, QBLK), :] = den
        mx12[pl.ds(q_off, QBLK), :] = mc_buf[...]

    run_pipeline(blocks1 + N_DIL4 * blocks4, a12, bias12, c12, 2 * QBLK)

    def rows3(u):
        return pl.ds(pl.multiple_of(u * QBLK, QBLK), QBLK)

    def a3(u, s_buf):
        stage_a(s_buf, QBLK, q3[rows3(u), :], k3[rows3(u), :])

    def c3(u, p_buf, mc_buf):
        acc, den = stage_c(p_buf, QBLK, v3[rows3(u), :])
        dst = pl.ds((u % N_DIL4) * n4 + u // N_DIL4, QBLK, stride=N_DIL4)
        acc3[dst, :] = acc
        den3[dst, :] = den
        mx3[dst, :] = mc_buf[...]

    run_pipeline(N_DIL16, a3, lambda u: bias_c[...], c3, QBLK)

    def merge(idx, carry):
        r = idx // blocks4
        c = idx % blocks4
        rn = pl.ds(c * QBLK * N_DIL4 + r, QBLK, stride=N_DIL4)
        r3 = pl.ds(pl.multiple_of(r * n4 + c * QBLK, QBLK), QBLK)
        r2 = pl.ds(pl.multiple_of(seq + r * n4 + c * QBLK, QBLK), QBLK)
        ma, mb, mc = mx12[rn, :], mx12[r2, :], mx3[r3, :]
        m = jnp.maximum(jnp.maximum(ma, mb), mc)
        wa, wb, wc = jnp.exp2(ma - m), jnp.exp2(mb - m), jnp.exp2(mc - m)
        num = wa * acc12[rn, :] + wb * acc12[r2, :] + wc * acc3[r3, :]
        den = wa * den12[rn, :] + wb * den12[r2, :] + wc * den3[r3, :]
        onat[rn, :] = num / den
        return carry

    lax.fori_loop(0, N_DIL4 * blocks4, merge, 0)
    o_ref[...] = onat[...].astype(o_ref.dtype)


def _attention(q, k, v):
    batch, pairs, seq, _ = q.shape
    assert seq == N_DIL16 * QBLK and DILATION_PAIRS == ((128, 1), (512, 4), (2048, 16))
    blk = pl.BlockSpec((None, None, seq, LANES), lambda j: (j // pairs, j % pairs, 0, 0))
    n4 = seq // N_DIL4
    pad_rows = QBLK + seq + N_DIL4 * (QBLK + n4)
    vm = pltpu.VMEM
    scratch = [
        vm((3, seq, LANES), F32), vm((3, seq, LANES), F32),
        vm((2 * seq, LANES), BF16), vm((pad_rows, LANES), BF16), vm((pad_rows, LANES), BF16),
        vm((seq, LANES), BF16), vm((seq, LANES), BF16), vm((seq, LANES), BF16),
        vm((2 * seq, LANES), F32), vm((2 * seq, LANES), F32), vm((2 * seq, LANES), F32),
        vm((seq, LANES), F32), vm((seq, LANES), F32), vm((seq, LANES), F32),
        vm((seq, LANES), F32),
        vm((2, QBLK, 2 * QBLK), F32), vm((QBLK, QBLK), F32),
        vm((SLOTS, QBLK, 4 * QBLK), F32), vm((SLOTS, QBLK, LANES), F32),
        vm((SLOTS, QBLK, 4 * QBLK), BF16),
    ]
    return pl.pallas_call(
        _attn_kernel,
        grid=(batch * pairs,),
        in_specs=[blk, blk, blk],
        out_specs=blk,
        out_shape=jax.ShapeDtypeStruct(q.shape, BF16),
        scratch_shapes=scratch,
        compiler_params=pltpu.CompilerParams(vmem_limit_bytes=VMEM_LIMIT),
    )(q, k, v)


def _ssm_kernel(u0_ref, un_ref, perm_ref, permt_ref, bh_ref, ch_ref, lam_ref, d_ref, wg_ref, bg_ref,
                gain_ref, o_ref, up0, up1, bu0, bu1, xs0, xs1, ys0, ys1, ya, gt, zb, st):
    batch, tb, sw = un_ref.shape
    half_states = bh_ref.shape[2] // 2
    half_ch = bh_ref.shape[1]
    cchunk = 256
    g = pl.program_id(0)

    chunks = [(h, c) for h in range(2) for c in range(0, half_states, cchunk)]

    def load_block(u_ref, up):
        up[...] = _dot(perm_ref[...], u_ref[...].reshape(batch * tb, sw)).astype(BF16)

    def expand_cols(up, bu, h, c):
        lhs = up[:, h * half_ch:(h + 1) * half_ch]
        for part in range(2):
            lo = part * half_states + c
            bu[:, h * 2 * half_states + lo:h * 2 * half_states + lo + cchunk] = _dot(
                lhs, bh_ref[h, :, lo:lo + cchunk])

    def scan_cols(bu, xs, h, c):
        for sub in range(c, c + cchunk, LANES):
            cre = slice(h * 2 * half_states + sub, h * 2 * half_states + sub + LANES)
            cim = slice(cre.start + half_states, cre.stop + half_states)
            cl = slice(h * half_states + sub, h * half_states + sub + LANES)
            lr = lam_ref[0:1, cl]
            li = lam_ref[1:2, cl]
            xr = st[:, cre]
            xi = st[:, cim]
            for i in range(tb):
                r = slice(i * batch, (i + 1) * batch)
                xr, xi = (lr * xr - li * xi + bu[r, cre], lr * xi + li * xr + bu[r, cim])
                xs[r, cre] = xr.astype(BF16)
                xs[r, cim] = xi.astype(BF16)
            st[:, cre] = xr
            st[:, cim] = xi

    def project_cols(xs, ys, h, c):
        out = slice(h * half_ch, (h + 1) * half_ch)
        acc = ys[:, out]
        for part in range(2):
            lo = part * half_states + c
            acc = acc + _dot(xs[:, h * 2 * half_states + lo:h * 2 * half_states + lo + cchunk],
                             ch_ref[h, lo:lo + cchunk, :])
        ys[:, out] = acc

    pieces = len(chunks) // 2
    piece_rows = batch * tb // pieces

    def act_rows(ys, i):
        r = slice(i * piece_rows, (i + 1) * piece_rows)
        ya[r, :] = jax.nn.gelu(ys[r, :])

    def gate_all():
        gt[...] = _dot(ya[...].astype(BF16), wg_ref[...]) + bg_ref[...]

    def norm_rows(i):
        r = slice(i * piece_rows, (i + 1) * piece_rows)
        z = ya[r, :] * (1.0 / (1.0 + jnp.exp(-gt[r, :])))
        z = z * lax.rsqrt(jnp.mean(z * z, axis=-1, keepdims=True) + RMS_EPS) * gain_ref[...]
        zb[r, :] = z.astype(BF16)

    def step(xs_q, ys_q, up_p, bu_p, xs_p, ys_p, up_q, bu_q):
        load_block(un_ref, up_q)
        for idx, (h, c) in enumerate(chunks):
            if idx < pieces:
                act_rows(ys_p, idx)
            else:
                if idx == pieces:
                    gate_all()
                    ys_p[...] = d_ref[...] * up_p[...].astype(F32)
                norm_rows(idx - pieces)
            project_cols(xs_q, ys_q, h, c)
            scan_cols(bu_p, xs_p, h, c)
            expand_cols(up_q, bu_q, h, c)
        o_ref[...] = _dot(permt_ref[...], zb[...]).astype(o_ref.dtype)

    @pl.when(g == 0)
    def _():
        st[...] = jnp.zeros_like(st)
        xs1[...] = jnp.zeros_like(xs1)
        ys0[...] = jnp.zeros_like(ys0)
        ys1[...] = jnp.zeros_like(ys1)
        load_block(u0_ref, up0)
        for h, c in chunks:
            expand_cols(up0, bu0, h, c)

    @pl.when(g % 2 == 0)
    def _():
        step(xs1, ys1, up0, bu0, xs0, ys0, up1, bu1)

    @pl.when(g % 2 == 1)
    def _():
        step(xs0, ys0, up1, bu1, xs1, ys1, up0, bu0)


def _ssm(u, perm, perm_t, l, bh, ch, lam, d_skip, w_glu, b_glu, gain):
    n_tblk, batch, tb, sw = u.shape
    assert n_tblk % 2 == 0
    m = batch * tb
    n_state = bh.shape[1] * bh.shape[3]
    full = lambda a: pl.BlockSpec(a.shape, lambda i: (0,) * a.ndim, pipeline_mode=pl.Buffered(1))
    consts = (bh, ch, lam, d_skip, w_glu, b_glu, gain)
    last = n_tblk - 1
    vm = pltpu.VMEM
    return pl.pallas_call(
        _ssm_kernel,
        grid=(n_tblk + 2,),
        in_specs=[pl.BlockSpec((None, batch, tb, sw), lambda i: (0, 0, 0, 0)),
                  pl.BlockSpec((None, batch, tb, sw), lambda i: (jnp.minimum(i + 1, last), 0, 0, 0)),
                  full(perm), full(perm_t)] + [_layer_spec(a, l, single_buffer=True) for a in consts],
        out_specs=pl.BlockSpec((m, sw), lambda i: (jnp.maximum(i - 2, 0), 0)),
        out_shape=jax.ShapeDtypeStruct((n_tblk * m, sw), BF16),
        scratch_shapes=[vm((m, sw), BF16), vm((m, sw), BF16),
                        vm((m, n_state), F32), vm((m, n_state), F32),
                        vm((m, n_state), BF16), vm((m, n_state), BF16),
                        vm((m, sw), F32), vm((m, sw), F32),
                        vm((m, sw), F32), vm((m, sw), F32), vm((m, sw), BF16),
                        vm((batch, n_state), F32)],
        compiler_params=pltpu.CompilerParams(dimension_semantics=("arbitrary",),
                                             vmem_limit_bytes=VMEM_LIMIT),
    )(u, u, perm, perm_t, *consts)


def _ssm_params(a_re, a_im, log_dt, b_re, b_im, c_re, c_im):
    depth, g, p = a_re.shape
    n = b_re.shape[-1]
    dt = jnp.exp(log_dt)[..., None]
    mag = jnp.exp(a_re * dt)
    ang = a_im * dt
    lb_re = mag * jnp.cos(ang)
    lb_im = mag * jnp.sin(ang)
    den = a_re * a_re + a_im * a_im
    nr = lb_re - 1.0
    ni = lb_im
    cr = (nr * a_re + ni * a_im) / den
    ci = (ni * a_re - nr * a_im) / den
    bb_re = cr[..., None] * b_re - ci[..., None] * b_im
    bb_im = cr[..., None] * b_im + ci[..., None] * b_re
    gh = g // 2
    eye = jnp.eye(gh, dtype=F32)
    halves = (slice(0, gh), slice(gh, g))

    def pack_b(bb):
        return jnp.einsum('lgpn,gh->lgnhp', bb, eye).reshape(depth, gh * n, gh * p)

    def pack_c(c):
        return jnp.einsum('lgnp,gh->lgphn', c, eye).reshape(depth, gh * p, gh * n)

    bh = jnp.stack([jnp.concatenate([pack_b(bb_re[:, s]), pack_b(bb_im[:, s])], axis=2)
                    for s in halves], axis=1)
    ch = jnp.stack([jnp.concatenate([pack_c(c_re[:, s]), -pack_c(c_im[:, s])], axis=1)
                    for s in halves], axis=1)
    lam = jnp.stack([lb_re.reshape(depth, -1), lb_im.reshape(depth, -1)], axis=1)
    return bh.astype(BF16), ch.astype(BF16), lam


def _layer_norm(x, g, b):
    mu = jnp.mean(x, axis=-1, keepdims=True)
    xc = x - mu
    var = jnp.mean(xc * xc, axis=-1, keepdims=True)
    return xc * lax.rsqrt(var + LN_EPS) * g + b


def _post_kernel(h_ref, a_ref, s_ref, ag_ref, wo_ref, bo_ref, g1_ref, b1_ref,
                 w1_ref, bf1_ref, w2_ref, bf2_ref, g2_ref, b2_ref, o_ref, *, alpha):
    nb, tb, d = h_ref.shape
    rows = nb * tb
    pairs = a_ref.shape[1]
    aw = pairs * LANES
    a = jnp.concatenate([a_ref[:, j].reshape(rows, LANES) for j in range(pairs)], axis=1).astype(F32)
    a = a * lax.rsqrt(jnp.mean(a * a, axis=-1, keepdims=True) + RMS_EPS) * ag_ref[...]
    mix = (_dot(a.astype(BF16), wo_ref[0:aw, :]) + _dot(s_ref[...], wo_ref[aw:, :]) + bo_ref[...])
    h1 = _layer_norm(alpha * h_ref[...].reshape(rows, d) + mix, g1_ref[...], b1_ref[...])
    h1b = h1.astype(BF16)
    ff = jnp.zeros_like(h1)
    for c in range(0, w1_ref.shape[1], FF_CHUNK):
        act = jnp.maximum(_dot(h1b, w1_ref[:, c:c + FF_CHUNK]) + bf1_ref[:, c:c + FF_CHUNK], 0.0)
        ff = ff + _dot((act * act).astype(BF16), w2_ref[c:c + FF_CHUNK, :])
    out = _layer_norm(alpha * h1 + ff + bf2_ref[...], g2_ref[...], b2_ref[...])
    o_ref[...] = out.reshape(nb, tb, d)


def _post(h, h_spec, attn, ssm, l, consts, alpha, out_shape, out_spec):
    batch, pairs, seq, _ = attn.shape
    n_tblk = seq // TBLK
    rows = batch * TBLK
    return pl.pallas_call(
        functools.partial(_post_kernel, alpha=alpha),
        grid=(n_tblk,),
        in_specs=[h_spec, pl.BlockSpec((batch, pairs, TBLK, LANES), lambda i: (0, 0, i, 0)),
                  pl.BlockSpec((rows, ssm.shape[1]), lambda i: (i, 0))]
                 + [_layer_spec(a, l, single_buffer=True) for a in consts],
        out_specs=out_spec,
        out_shape=out_shape,
        compiler_params=pltpu.CompilerParams(vmem_limit_bytes=VMEM_LIMIT),
    )(h, attn, ssm, *consts)


def kernel(x, positions, w_in, attn_gain, ssm_gain, ssm_a_re, ssm_a_im, ssm_log_dt, ssm_b_re,
           ssm_b_im, ssm_c_re, ssm_c_im, ssm_d, w_glu, b_glu, w_out, b_out, ln1_g, ln1_b,
           w_ff1, b_ff1, w_ff2, b_ff2, ln2_g, ln2_b):
    batch, seq, d = x.shape
    depth = w_in.shape[0]
    aw = attn_gain.shape[1]
    t = batch * seq
    n_tblk = seq // TBLK
    rows = batch * TBLK
    alpha = (2.0 * depth) ** 0.25
    vec = lambda a: a.reshape(depth, 1, -1).astype(F32)

    half = HEAD_DIM // 2
    inv_freq = ROPE_THETA ** (-jnp.arange(half, dtype=F32) * 2.0 / HEAD_DIM)
    freq_row = jnp.tile(inv_freq, LANES // half).reshape(1, LANES)
    sign_row = jnp.tile(jnp.concatenate([-jnp.ones(half, F32), jnp.ones(half, F32)]),
                        LANES // HEAD_DIM).reshape(1, LANES)
    pos_col = positions.reshape(batch, n_tblk, TBLK).transpose(1, 0, 2).reshape(t, 1).astype(F32)
    cos_t, sin_t = _rope_tables(pos_col, freq_row, sign_row)

    r_out = jnp.arange(rows, dtype=jnp.int32)
    r_in = (r_out % batch) * TBLK + r_out // batch
    perm = (r_in[:, None] == jnp.arange(rows, dtype=jnp.int32)[None, :]).astype(BF16)
    perm_t = perm.T

    w_in_b = w_in.astype(BF16)
    bh, ch, lam = _ssm_params(ssm_a_re, ssm_a_im, ssm_log_dt, ssm_b_re, ssm_b_im, ssm_c_re, ssm_c_im)
    ssm_consts = (bh, ch, lam, vec(ssm_d), w_glu.astype(BF16), vec(b_glu), vec(ssm_gain))
    post_consts = (vec(attn_gain), w_out.astype(BF16), vec(b_out), vec(ln1_g), vec(ln1_b),
                   w_ff1.astype(BF16), vec(b_ff1), w_ff2.astype(BF16), vec(b_ff2), vec(ln2_g),
                   vec(ln2_b))

    by_seq_spec = pl.BlockSpec((batch, None, TBLK, d), lambda i: (0, i, 0, 0))
    by_blk_spec = pl.BlockSpec((batch, TBLK, d), lambda i: (i, 0, 0))
    h = x.reshape(batch, n_tblk, TBLK, d)
    h_spec = by_seq_spec
    for l in range(depth):
        q, k, v, u = _proj(h, h_spec, w_in_b, l, cos_t, sin_t, aw, batch, n_tblk)
        attn = _attention(q, k, v)
        ssm = _ssm(u, perm, perm_t, l, *ssm_consts)
        if l == depth - 1:
            out_shape, out_spec = jax.ShapeDtypeStruct((batch, n_tblk, TBLK, d), F32), by_seq_spec
        else:
            out_shape, out_spec = jax.ShapeDtypeStruct((n_tblk * batch, TBLK, d), F32), by_blk_spec
        h = _post(h, h_spec, attn, ssm, l, post_consts, alpha, out_shape, out_spec)
        h_spec = by_blk_spec
    return h.reshape(batch, seq, d)
```

```python
import functools
import math

import jax
import jax.numpy as jnp
from jax import lax
from jax.experimental import pallas as pl
from jax.experimental.pallas import tpu as pltpu

F32 = jnp.float32
BF16 = jnp.bfloat16

HEAD_DIM = 64
LANES = 128
DILATION_PAIRS = ((128, 1), (512, 4), (2048, 16))
QBLK = 128
N_DIL4 = 4
N_DIL16 = 16
GROUP = 8
SLOTS = 2 * GROUP
ROPE_THETA = 10000.0
LN_EPS = 1e-5
RMS_EPS = 1e-6
NEG_INF = -1e30
LOG2E = math.log2(math.e)
VMEM_LIMIT = 56 * 1024 * 1024

TBLK = 16
FF_CHUNK = 1024


def _dot(a, b):
    return jnp.dot(a, b, preferred_element_type=F32)


def _dot_nt(a, b):
    return lax.dot_general(a, b, (((1,), (1,)), ((), ())), preferred_element_type=F32)


def _layer_spec(arr, l, single_buffer=False):
    mode = dict(pipeline_mode=pl.Buffered(1)) if single_buffer else {}
    return pl.BlockSpec((None,) + arr.shape[1:], lambda i: (l,) + (0,) * (arr.ndim - 1), **mode)


def _rope_table_kernel(pos_ref, freq_ref, sign_ref, cos_ref, sin_ref):
    ang = pos_ref[...] * freq_ref[...]
    cos_ref[...] = jnp.cos(ang)
    sin_ref[...] = jnp.sin(ang) * sign_ref[...]


def _rope_tables(pos_col, freq_row, sign_row):
    t = pos_col.shape[0]
    tm = 2048
    row = pl.BlockSpec((1, LANES), lambda i: (0, 0))
    tab = pl.BlockSpec((tm, LANES), lambda i: (i, 0))
    return pl.pallas_call(
        _rope_table_kernel,
        grid=(t // tm,),
        in_specs=[pl.BlockSpec((tm, 1), lambda i: (i, 0)), row, row],
        out_specs=[tab, tab],
        out_shape=[jax.ShapeDtypeStruct((t, LANES), F32)] * 2,
    )(pos_col, freq_row, sign_row)


def _proj_kernel(h_ref, w_ref, cos_ref, sin_ref, q_ref, k_ref, v_ref, u_ref, *, aw):
    nb, tb, d = h_ref.shape
    rows = nb * tb
    hb = h_ref[...].reshape(rows, d).astype(BF16)
    cos = cos_ref[...]
    sin = sin_ref[...]
    lane = lax.broadcasted_iota(jnp.int32, (1, LANES), 1)
    first_half = (lane % HEAD_DIM) < (HEAD_DIM // 2)

    def rope(x):
        rot = jnp.where(first_half,
                        pltpu.roll(x, LANES - HEAD_DIM // 2, 1),
                        pltpu.roll(x, HEAD_DIM // 2, 1))
        return x * cos + rot * sin

    def put(ref, j, val):
        ref[:, j] = val.astype(BF16).reshape(nb, tb, LANES)

    q = _dot(hb, w_ref[:, 0:aw])
    k = _dot(hb, w_ref[:, aw:2 * aw])
    v = _dot(hb, w_ref[:, 2 * aw:3 * aw])
    scale = HEAD_DIM ** -0.5 * LOG2E
    for j in range(aw // LANES):
        sl = slice(j * LANES, (j + 1) * LANES)
        put(q_ref, j, rope(q[:, sl]) * scale)
        put(k_ref, j, rope(k[:, sl]))
        put(v_ref, j, v[:, sl])
    u_ref[...] = _dot(hb, w_ref[:, 3 * aw:]).astype(BF16).reshape(u_ref.shape)


def _proj(h, h_spec, w_in, l, cos_t, sin_t, aw, batch, n_tblk):
    d, n = w_in.shape[1:]
    sw = n - 3 * aw
    rows = batch * TBLK
    tab = pl.BlockSpec((rows, LANES), lambda i: (i, 0))
    pairs = aw // LANES
    qkv_spec = pl.BlockSpec((batch, pairs, TBLK, LANES), lambda i: (0, 0, i, 0))
    qkv_shape = jax.ShapeDtypeStruct((batch, pairs, n_tblk * TBLK, LANES), BF16)
    return pl.pallas_call(
        functools.partial(_proj_kernel, aw=aw),
        grid=(n_tblk,),
        in_specs=[h_spec, _layer_spec(w_in, l), tab, tab],
        out_specs=[qkv_spec] * 3 + [pl.BlockSpec((None, batch, TBLK, sw), lambda i: (i, 0, 0, 0))],
        out_shape=[qkv_shape] * 3 + [jax.ShapeDtypeStruct((n_tblk, batch, TBLK, sw), BF16)],
        compiler_params=pltpu.CompilerParams(vmem_limit_bytes=VMEM_LIMIT),
    )(h, w_in, cos_t, sin_t)


def _attn_kernel(q_ref, k_ref, v_ref, o_ref,
                 xf, x4, q12, k12, v12, q3, k3, v3, acc12, den12, mx12, acc3, den3, mx3, onat,
                 bias2, bias_c, s_all, mc_all, p_all):
    seq = q_ref.shape[0]
    n4 = seq // N_DIL4
    pad_blk = QBLK + n4
    base2 = QBLK + seq

    head0 = lax.broadcasted_iota(jnp.int32, (1, LANES), 1) < HEAD_DIM
    ones_h0 = jnp.where(head0, 1.0, 0.0).astype(BF16)
    ones_h1 = jnp.where(head0, 0.0, 1.0).astype(BF16)

    qi = lax.broadcasted_iota(jnp.int32, (QBLK, 2 * QBLK), 0)
    kj = lax.broadcasted_iota(jnp.int32, (QBLK, 2 * QBLK), 1)
    band = (kj >= qi) & (kj <= qi + QBLK)
    bias2[0] = jnp.where(band, 0.0, NEG_INF).astype(F32)
    bias2[1] = jnp.where(band & (kj >= QBLK), 0.0, NEG_INF).astype(F32)
    qi = lax.broadcasted_iota(jnp.int32, (QBLK, QBLK), 0)
    kj = lax.broadcasted_iota(jnp.int32, (QBLK, QBLK), 1)
    bias_c[...] = jnp.where(kj <= qi, 0.0, NEG_INF).astype(F32)

    nat = [ref[...] for ref in (q_ref, k_ref, v_ref)]
    for i in range(3):
        xf[i] = nat[i].astype(F32)
    for r in range(N_DIL4):
        for i in range(3):
            x4[i, r * n4:(r + 1) * n4, :] = xf[i, pl.ds(r, n4, stride=N_DIL4), :]
    q12[0:seq, :] = nat[0]
    zero_blk = jnp.zeros((QBLK, LANES), BF16)
    for dst, i in ((k12, 1), (v12, 2)):
        dst[0:QBLK, :] = zero_blk
        dst[QBLK:base2, :] = nat[i]
        for r in range(N_DIL4):
            b = base2 + r * pad_blk
            dst[b:b + QBLK, :] = zero_blk
            dst[b + QBLK:b + pad_blk, :] = x4[i, r * n4:(r + 1) * n4, :].astype(BF16)
    for r in range(N_DIL4):
        q12[seq + r * n4:seq + (r + 1) * n4, :] = x4[0, r * n4:(r + 1) * n4, :].astype(BF16)
    for r16 in range(N_DIL16):
        rows = pl.ds((r16 % N_DIL4) * n4 + r16 // N_DIL4, QBLK, stride=N_DIL4)
        for dst, i in ((q3, 0), (k3, 1), (v3, 2)):
            dst[r16 * QBLK:(r16 + 1) * QBLK, :] = x4[i, rows, :].astype(BF16)

    def stage_a(s_buf, nk, qb, kb):
        zero = jnp.zeros_like(kb)
        kblk = jnp.concatenate([jnp.where(head0, kb, zero), jnp.where(head0, zero, kb)], axis=0)
        s_buf[:, 0:2 * nk] = _dot_nt(qb, kblk)

    def stage_b(s_buf, mc_buf, p_buf, nk, bias):
        ms = []
        for a in range(2):
            sa = s_buf[:, a * nk:(a + 1) * nk] + bias
            m = jnp.max(sa, axis=-1, keepdims=True)
            p_buf[:, a * nk:(a + 1) * nk] = jnp.exp2(sa - m).astype(BF16)
            ms.append(jnp.broadcast_to(m, (QBLK, LANES)))
        mc_buf[...] = jnp.where(head0, ms[0], ms[1])

    def stage_c(p_buf, nk, vb):
        zero = jnp.zeros_like(vb)
        w = jnp.concatenate(
            [jnp.concatenate([jnp.where(head0, vb, zero), jnp.broadcast_to(ones_h0, vb.shape)], axis=1),
             jnp.concatenate([jnp.where(head0, zero, vb), jnp.broadcast_to(ones_h1, vb.shape)], axis=1)],
            axis=0)
        r = _dot(p_buf[:, 0:2 * nk], w)
        return r[:, :LANES], r[:, LANES:]

    def run_pipeline(n_units, a_fn, bias_fn, c_fn, nk):
        def a(u):
            u = jnp.int32(u)
            a_fn(u, s_all.at[u % SLOTS])

        def b(u):
            u = jnp.int32(u)
            slot = u % SLOTS
            stage_b(s_all.at[slot], mc_all.at[slot], p_all.at[slot], nk, bias_fn(u))

        def c(u):
            u = jnp.int32(u)
            slot = u % SLOTS
            c_fn(u, p_all.at[slot], mc_all.at[slot])

        g = GROUP
        for u in range(2 * g):
            a(u)
        for u in range(g):
            b(u)

        def body(j, carry):
            u0 = g * j
            for i in range(g):
                c(u0 + i)
            for i in range(g):
                b(u0 + g + i)
            for i in range(g):
                a(u0 + 2 * g + i)
            return carry

        lax.fori_loop(0, n_units // g - 2, body, 0)
        for u in range(n_units - 2 * g, n_units - g):
            c(u)
        for u in range(n_units - g, n_units):
            b(u)
        for u in range(n_units - g, n_units):
            c(u)

    blocks1 = seq // QBLK
    blocks4 = n4 // QBLK

    def offsets12(u):
        v = u - blocks1
        r = v // blocks4
        blk = v % blocks4
        is_nat = u < blocks1
        q_off = jnp.where(is_nat, u * QBLK, seq + v * QBLK)
        k_off = jnp.where(is_nat, u * QBLK, base2 + r * pad_blk + blk * QBLK)
        first = jnp.where(is_nat, u == 0, blk == 0)
        return pl.multiple_of(q_off, QBLK), pl.multiple_of(k_off, QBLK), first.astype(jnp.int32)

    def a12(u, s_buf):
        q_off, k_off, _ = offsets12(u)
        stage_a(s_buf, 2 * QBLK, q12[pl.ds(q_off, QBLK), :], k12[pl.ds(k_off, 2 * QBLK), :])

    def bias12(u):
        return bias2[offsets12(u)[2]]

    def c12(u, p_buf, mc_buf):
        q_off, k_off, _ = offsets12(u)
        acc, den = stage_c(p_buf, 2 * QBLK, v12[pl.ds(k_off, 2 * QBLK), :])
        acc12[pl.ds(q_off, QBLK), :] = acc
        den12[pl.ds(q_off, QBLK), :] = den
        mx12[pl.ds(q_off, QBLK), :] = mc_buf[...]

    run_pipeline(blocks1 + N_DIL4 * blocks4, a12, bias12, c12, 2 * QBLK)

    def rows3(u):
        return pl.ds(pl.multiple_of(u * QBLK, QBLK), QBLK)

    def a3(u, s_buf):
        stage_a(s_buf, QBLK, q3[rows3(u), :], k3[rows3(u), :])

    def c3(u, p_buf, mc_buf):
        acc, den = stage_c(p_buf, QBLK, v3[rows3(u), :])
        dst = pl.ds((u % N_DIL4) * n4 + u // N_DIL4, QBLK, stride=N_DIL4)
        acc3[dst, :] = acc
        den3[dst, :] = den
        mx3[dst, :] = mc_buf[...]

    run_pipeline(N_DIL16, a3, lambda u: bias_c[...], c3, QBLK)

    def merge(idx, carry):
        r = idx // blocks4
        c = idx % blocks4
        rn = pl.ds(c * QBLK * N_DIL4 + r, QBLK, stride=N_DIL4)
        r3 = pl.ds(pl.multiple_of(r * n4 + c * QBLK, QBLK), QBLK)
        r2 = pl.ds(pl.multiple_of(seq + r * n4 + c * QBLK, QBLK), QBLK)
        ma, mb, mc = mx12[rn, :], mx12[r2, :], mx3[r3, :]
        m = jnp.maximum(jnp.maximum(ma, mb), mc)
        wa, wb, wc = jnp.exp2(ma - m), jnp.exp2(mb - m), jnp.exp2(mc - m)
        num = wa * acc12[rn, :] + wb * acc12[r2, :] + wc * acc3[r3, :]
        den = wa * den12[rn, :] + wb * den12[r2, :] + wc * den3[r3, :]
        onat[rn, :] = num / den
        return carry

    lax.fori_loop(0, N_DIL4 * blocks4, merge, 0)
    o_ref[...] = onat[...].astype(o_ref.dtype)


def _attention(q, k, v):
    batch, pairs, seq, _ = q.shape
    assert seq == N_DIL16 * QBLK and DILATION_PAIRS == ((128, 1), (512, 4), (2048, 16))
    blk = pl.BlockSpec((None, None, seq, LANES), lambda j: (j // pairs, j % pairs, 0, 0))
    n4 = seq // N_DIL4
    pad_rows = QBLK + seq + N_DIL4 * (QBLK + n4)
    vm = pltpu.VMEM
    scratch = [
        vm((3, seq, LANES), F32), vm((3, seq, LANES), F32),
        vm((2 * seq, LANES), BF16), vm((pad_rows, LANES), BF16), vm((pad_rows, LANES), BF16),
        vm((seq, LANES), BF16), vm((seq, LANES), BF16), vm((seq, LANES), BF16),
        vm((2 * seq, LANES), F32), vm((2 * seq, LANES), F32), vm((2 * seq, LANES), F32),
        vm((seq, LANES), F32), vm((seq, LANES), F32), vm((seq, LANES), F32),
        vm((seq, LANES), F32),
        vm((2, QBLK, 2 * QBLK), F32), vm((QBLK, QBLK), F32),
        vm((SLOTS, QBLK, 4 * QBLK), F32), vm((SLOTS, QBLK, LANES), F32),
        vm((SLOTS, QBLK, 4 * QBLK), BF16),
    ]
    return pl.pallas_call(
        _attn_kernel,
        grid=(batch * pairs,),
        in_specs=[blk, blk, blk],
        out_specs=blk,
        out_shape=jax.ShapeDtypeStruct(q.shape, BF16),
        scratch_shapes=scratch,
        compiler_params=pltpu.CompilerParams(vmem_limit_bytes=VMEM_LIMIT),
    )(q, k, v)


def _ssm_kernel(u0_ref, un_ref, perm_ref, permt_ref, bh_ref, ch_ref, lam_ref, d_ref, wg_ref, bg_ref,
                gain_ref, o_ref, up0, up1, bu0, bu1, xs0, xs1, ys0, ys1, ya, gt, zb, st):
    batch, tb, sw = un_ref.shape
    half_states = bh_ref.shape[2] // 2
    half_ch = bh_ref.shape[1]
    cchunk = 256
    g = pl.program_id(0)

    chunks = [(h, c) for h in range(2) for c in range(0, half_states, cchunk)]

    def load_block(u_ref, up):
        up[...] = _dot(perm_ref[...], u_ref[...].reshape(batch * tb, sw)).astype(BF16)

    def expand_cols(up, bu, h, c):
        lhs = up[:, h * half_ch:(h + 1) * half_ch]
        for part in range(2):
            lo = part * half_states + c
            bu[:, h * 2 * half_states + lo:h * 2 * half_states + lo + cchunk] = _dot(
                lhs, bh_ref[h, :, lo:lo + cchunk])

    def scan_cols(bu, xs, h, c):
        for sub in range(c, c + cchunk, LANES):
            cre = slice(h * 2 * half_states + sub, h * 2 * half_states + sub + LANES)
            cim = slice(cre.start + half_states, cre.stop + half_states)
            cl = slice(h * half_states + sub, h * half_states + sub + LANES)
            lr = lam_ref[0:1, cl]
            li = lam_ref[1:2, cl]
            xr = st[:, cre]
            xi = st[:, cim]
            for i in range(tb):
                r = slice(i * batch, (i + 1) * batch)
                xr, xi = (lr * xr - li * xi + bu[r, cre], lr * xi + li * xr + bu[r, cim])
                xs[r, cre] = xr.astype(BF16)
                xs[r, cim] = xi.astype(BF16)
            st[:, cre] = xr
            st[:, cim] = xi

    def project_cols(xs, ys, h, c):
        out = slice(h * half_ch, (h + 1) * half_ch)
        acc = ys[:, out]
        for part in range(2):
            lo = part * half_states + c
            acc = acc + _dot(xs[:, h * 2 * half_states + lo:h * 2 * half_states + lo + cchunk],
                             ch_ref[h, lo:lo + cchunk, :])
        ys[:, out] = acc

    pieces = len(chunks) // 2
    piece_rows = batch * tb // pieces

    def act_rows(ys, i):
        r = slice(i * piece_rows, (i + 1) * piece_rows)
        ya[r, :] = jax.nn.gelu(ys[r, :])

    def gate_all():
        gt[...] = _dot(ya[...].astype(BF16), wg_ref[...]) + bg_ref[...]

    def norm_rows(i):
        r = slice(i * piece_rows, (i + 1) * piece_rows)
        z = ya[r, :] * (1.0 / (1.0 + jnp.exp(-gt[r, :])))
        z = z * lax.rsqrt(jnp.mean(z * z, axis=-1, keepdims=True) + RMS_EPS) * gain_ref[...]
        zb[r, :] = z.astype(BF16)

    def step(xs_q, ys_q, up_p, bu_p, xs_p, ys_p, up_q, bu_q):
        load_block(un_ref, up_q)
        for idx, (h, c) in enumerate(chunks):
            if idx < pieces:
                act_rows(ys_p, idx)
            else:
                if idx == pieces:
                    gate_all()
                    ys_p[...] = d_ref[...] * up_p[...].astype(F32)
                norm_rows(idx - pieces)
            project_cols(xs_q, ys_q, h, c)
            scan_cols(bu_p, xs_p, h, c)
            expand_cols(up_q, bu_q, h, c)
        o_ref[...] = _dot(permt_ref[...], zb[...]).astype(o_ref.dtype)

    @pl.when(g == 0)
    def _():
        st[...] = jnp.zeros_like(st)
        xs1[...] = jnp.zeros_like(xs1)
        ys0[...] = jnp.zeros_like(ys0)
        ys1[...] = jnp.zeros_like(ys1)
        load_block(u0_ref, up0)
        for h, c in chunks:
            expand_cols(up0, bu0, h, c)

    @pl.when(g % 2 == 0)
    def _():
        step(xs1, ys1, up0, bu0, xs0, ys0, up1, bu1)

    @pl.when(g % 2 == 1)
    def _():
        step(xs0, ys0, up1, bu1, xs1, ys1, up0, bu0)


def _ssm(u, perm, perm_t, l, bh, ch, lam, d_skip, w_glu, b_glu, gain):
    n_tblk, batch, tb, sw = u.shape
    assert n_tblk % 2 == 0
    m = batch * tb
    n_state = bh.shape[1] * bh.shape[3]
    full = lambda a: pl.BlockSpec(a.shape, lambda i: (0,) * a.ndim, pipeline_mode=pl.Buffered(1))
    consts = (bh, ch, lam, d_skip, w_glu, b_glu, gain)
    last = n_tblk - 1
    vm = pltpu.VMEM
    return pl.pallas_call(
        _ssm_kernel,
        grid=(n_tblk + 2,),
        in_specs=[pl.BlockSpec((None, batch, tb, sw), lambda i: (0, 0, 0, 0)),
                  pl.BlockSpec((None, batch, tb, sw), lambda i: (jnp.minimum(i + 1, last), 0, 0, 0)),
                  full(perm), full(perm_t)] + [_layer_spec(a, l, single_buffer=True) for a in consts],
        out_specs=pl.BlockSpec((m, sw), lambda i: (jnp.maximum(i - 2, 0), 0)),
        out_shape=jax.ShapeDtypeStruct((n_tblk * m, sw), BF16),
        scratch_shapes=[vm((m, sw), BF16), vm((m, sw), BF16),
                        vm((m, n_state), F32), vm((m, n_state), F32),
                        vm((m, n_state), BF16), vm((m, n_state), BF16),
                        vm((m, sw), F32), vm((m, sw), F32),
                        vm((m, sw), F32), vm((m, sw), F32), vm((m, sw), BF16),
                        vm((batch, n_state), F32)],
        compiler_params=pltpu.CompilerParams(dimension_semantics=("arbitrary",),
                                             vmem_limit_bytes=VMEM_LIMIT),
    )(u, u, perm, perm_t, *consts)


def _ssm_params(a_re, a_im, log_dt, b_re, b_im, c_re, c_im):
    depth, g, p = a_re.shape
    n = b_re.shape[-1]
    dt = jnp.exp(log_dt)[..., None]
    mag = jnp.exp(a_re * dt)
    ang = a_im * dt
    lb_re = mag * jnp.cos(ang)
    lb_im = mag * jnp.sin(ang)
    den = a_re * a_re + a_im * a_im
    nr = lb_re - 1.0
    ni = lb_im
    cr = (nr * a_re + ni * a_im) / den
    ci = (ni * a_re - nr * a_im) / den
    bb_re = cr[..., None] * b_re - ci[..., None] * b_im
    bb_im = cr[..., None] * b_im + ci[..., None] * b_re
    gh = g // 2
    eye = jnp.eye(gh, dtype=F32)
    halves = (slice(0, gh), slice(gh, g))

    def pack_b(bb):
        return jnp.einsum('lgpn,gh->lgnhp', bb, eye).reshape(depth, gh * n, gh * p)

    def pack_c(c):
        return jnp.einsum('lgnp,gh->lgphn', c, eye).reshape(depth, gh * p, gh * n)

    bh = jnp.stack([jnp.concatenate([pack_b(bb_re[:, s]), pack_b(bb_im[:, s])], axis=2)
                    for s in halves], axis=1)
    ch = jnp.stack([jnp.concatenate([pack_c(c_re[:, s]), -pack_c(c_im[:, s])], axis=1)
                    for s in halves], axis=1)
    lam = jnp.stack([lb_re.reshape(depth, -1), lb_im.reshape(depth, -1)], axis=1)
    return bh.astype(BF16), ch.astype(BF16), lam


def _layer_norm(x, g, b):
    mu = jnp.mean(x, axis=-1, keepdims=True)
    xc = x - mu
    var = jnp.mean(xc * xc, axis=-1, keepdims=True)
    return xc * lax.rsqrt(var + LN_EPS) * g + b


def _post_kernel(h_ref, a_ref, s_ref, ag_ref, wo_ref, bo_ref, g1_ref, b1_ref,
                 w1_ref, bf1_ref, w2_ref, bf2_ref, g2_ref, b2_ref, o_ref,
                 h1f0, h1f1, h1b0, h1b1, pre0, pre1, *, alpha):
    nb, tb, d = h_ref.shape
    rows = nb * tb
    pairs = a_ref.shape[1]
    aw = pairs * LANES
    g = pl.program_id(0)
    ff_chunks = range(0, w1_ref.shape[1], FF_CHUNK)

    def mix():
        a = jnp.concatenate([a_ref[:, j].reshape(rows, LANES) for j in range(pairs)],
                            axis=1).astype(F32)
        a = a * lax.rsqrt(jnp.mean(a * a, axis=-1, keepdims=True) + RMS_EPS) * ag_ref[...]
        return _dot(a.astype(BF16), wo_ref[0:aw, :]) + _dot(s_ref[...], wo_ref[aw:, :]) + bo_ref[...]

    def norm1(mixed, h1f, h1b):
        h1 = _layer_norm(alpha * h_ref[...].reshape(rows, d) + mixed, g1_ref[...], b1_ref[...])
        h1f[...] = h1
        h1b[...] = h1.astype(BF16)

    def ff_chunk(h1b, c):
        act = jnp.maximum(_dot(h1b[...], w1_ref[:, c:c + FF_CHUNK]) + bf1_ref[:, c:c + FF_CHUNK], 0.0)
        return _dot((act * act).astype(BF16), w2_ref[c:c + FF_CHUNK, :])

    def step(h1f_q, h1b_q, pre_q, h1f_p, h1b_p, pre_p):
        mixed = mix()
        ff = ff_chunk(h1b_q, ff_chunks[0])
        norm1(mixed, h1f_p, h1b_p)
        ff = ff + ff_chunk(h1b_q, ff_chunks[1])
        o_ref[...] = _layer_norm(pre_p[...], g2_ref[...], b2_ref[...]).reshape(nb, tb, d)
        for c in ff_chunks[2:]:
            ff = ff + ff_chunk(h1b_q, c)
        pre_q[...] = alpha * h1f_q[...] + ff + bf2_ref[...]

    @pl.when(g == 0)
    def _():
        h1f1[...] = jnp.zeros_like(h1f1)
        h1b1[...] = jnp.zeros_like(h1b1)
        pre0[...] = jnp.zeros_like(pre0)

    @pl.when(g % 2 == 0)
    def _():
        step(h1f1, h1b1, pre1, h1f0, h1b0, pre0)

    @pl.when(g % 2 == 1)
    def _():
        step(h1f0, h1b0, pre0, h1f1, h1b1, pre1)


def _post(h, h_by, attn, ssm, l, consts, alpha, out_shape, out_by):
    batch, pairs, seq, _ = attn.shape
    n_tblk = seq // TBLK
    assert n_tblk % 2 == 0
    rows = batch * TBLK
    d = h.shape[-1]
    this = lambda i: jnp.minimum(i, n_tblk - 1)
    done = lambda i: jnp.maximum(i - 2, 0)
    return pl.pallas_call(
        functools.partial(_post_kernel, alpha=alpha),
        grid=(n_tblk + 2,),
        in_specs=[h_by(this),
                  pl.BlockSpec((batch, pairs, TBLK, LANES), lambda i: (0, 0, this(i), 0)),
                  pl.BlockSpec((rows, ssm.shape[1]), lambda i: (this(i), 0))]
                 + [_layer_spec(a, l, single_buffer=True) for a in consts],
        out_specs=out_by(done),
        out_shape=out_shape,
        scratch_shapes=[pltpu.VMEM((rows, d), F32), pltpu.VMEM((rows, d), F32),
                        pltpu.VMEM((rows, d), BF16), pltpu.VMEM((rows, d), BF16),
                        pltpu.VMEM((rows, d), F32), pltpu.VMEM((rows, d), F32)],
        compiler_params=pltpu.CompilerParams(dimension_semantics=("arbitrary",),
                                             vmem_limit_bytes=VMEM_LIMIT),
    )(h, attn, ssm, *consts)


def kernel(x, positions, w_in, attn_gain, ssm_gain, ssm_a_re, ssm_a_im, ssm_log_dt, ssm_b_re,
           ssm_b_im, ssm_c_re, ssm_c_im, ssm_d, w_glu, b_glu, w_out, b_out, ln1_g, ln1_b,
           w_ff1, b_ff1, w_ff2, b_ff2, ln2_g, ln2_b):
    batch, seq, d = x.shape
    depth = w_in.shape[0]
    aw = attn_gain.shape[1]
    t = batch * seq
    n_tblk = seq // TBLK
    rows = batch * TBLK
    alpha = (2.0 * depth) ** 0.25
    vec = lambda a: a.reshape(depth, 1, -1).astype(F32)

    half = HEAD_DIM // 2
    inv_freq = ROPE_THETA ** (-jnp.arange(half, dtype=F32) * 2.0 / HEAD_DIM)
    freq_row = jnp.tile(inv_freq, LANES // half).reshape(1, LANES)
    sign_row = jnp.tile(jnp.concatenate([-jnp.ones(half, F32), jnp.ones(half, F32)]),
                        LANES // HEAD_DIM).reshape(1, LANES)
    pos_col = positions.reshape(batch, n_tblk, TBLK).transpose(1, 0, 2).reshape(t, 1).astype(F32)
    cos_t, sin_t = _rope_tables(pos_col, freq_row, sign_row)

    r_out = jnp.arange(rows, dtype=jnp.int32)
    r_in = (r_out % batch) * TBLK + r_out // batch
    perm = (r_in[:, None] == jnp.arange(rows, dtype=jnp.int32)[None, :]).astype(BF16)
    perm_t = perm.T

    w_in_b = w_in.astype(BF16)
    bh, ch, lam = _ssm_params(ssm_a_re, ssm_a_im, ssm_log_dt, ssm_b_re, ssm_b_im, ssm_c_re, ssm_c_im)
    ssm_consts = (bh, ch, lam, vec(ssm_d), w_glu.astype(BF16), vec(b_glu), vec(ssm_gain))
    post_consts = (vec(attn_gain), w_out.astype(BF16), vec(b_out), vec(ln1_g), vec(ln1_b),
                   w_ff1.astype(BF16), vec(b_ff1), w_ff2.astype(BF16), vec(b_ff2), vec(ln2_g),
                   vec(ln2_b))

    by_seq = lambda blk: pl.BlockSpec((batch, None, TBLK, d), lambda i: (0, blk(i), 0, 0))
    by_blk = lambda blk: pl.BlockSpec((batch, TBLK, d), lambda i: (blk(i), 0, 0))
    h = x.reshape(batch, n_tblk, TBLK, d)
    h_by = by_seq
    for l in range(depth):
        q, k, v, u = _proj(h, h_by(lambda i: i), w_in_b, l, cos_t, sin_t, aw, batch, n_tblk)
        attn = _attention(q, k, v)
        ssm = _ssm(u, perm, perm_t, l, *ssm_consts)
        if l == depth - 1:
            out_shape, out_by = jax.ShapeDtypeStruct((batch, n_tblk, TBLK, d), F32), by_seq
        else:
            out_shape, out_by = jax.ShapeDtypeStruct((n_tblk * batch, TBLK, d), F32), by_blk
        h = _post(h, h_by, attn, ssm, l, post_consts, alpha, out_shape, out_by)
        h_by = by_blk
    return h.reshape(batch, seq, d)
```

```python
import functools
import math

import jax
import jax.numpy as jnp
from jax import lax
from jax.experimental import pallas as pl
from jax.experimental.pallas import tpu as pltpu

F32 = jnp.float32
BF16 = jnp.bfloat16

HEAD_DIM = 64
LANES = 128
DILATION_PAIRS = ((128, 1), (512, 4), (2048, 16))
QBLK = 128
N_DIL4 = 4
N_DIL16 = 16
GROUP = 8
SLOTS = 2 * GROUP
ROPE_THETA = 10000.0
LN_EPS = 1e-5
RMS_EPS = 1e-6
NEG_INF = -1e30
LOG2E = math.log2(math.e)
VMEM_LIMIT = 56 * 1024 * 1024

TBLK = 16
FF_CHUNK = 1024


def _dot(a, b):
    return jnp.dot(a, b, preferred_element_type=F32)


def _dot_nt(a, b):
    return lax.dot_general(a, b, (((1,), (1,)), ((), ())), preferred_element_type=F32)


def _layer_spec(arr, l, single_buffer=False):
    mode = dict(pipeline_mode=pl.Buffered(1)) if single_buffer else {}
    return pl.BlockSpec((None,) + arr.shape[1:], lambda i: (l,) + (0,) * (arr.ndim - 1), **mode)


def _rope_table_kernel(pos_ref, freq_ref, sign_ref, cos_ref, sin_ref):
    ang = pos_ref[...] * freq_ref[...]
    cos_ref[...] = jnp.cos(ang)
    sin_ref[...] = jnp.sin(ang) * sign_ref[...]


def _rope_tables(pos_col, freq_row, sign_row):
    t = pos_col.shape[0]
    tm = 2048
    row = pl.BlockSpec((1, LANES), lambda i: (0, 0))
    tab = pl.BlockSpec((tm, LANES), lambda i: (i, 0))
    return pl.pallas_call(
        _rope_table_kernel,
        grid=(t // tm,),
        in_specs=[pl.BlockSpec((tm, 1), lambda i: (i, 0)), row, row],
        out_specs=[tab, tab],
        out_shape=[jax.ShapeDtypeStruct((t, LANES), F32)] * 2,
    )(pos_col, freq_row, sign_row)


def _proj_kernel(h_ref, w_ref, cos_ref, sin_ref, q_ref, k_ref, v_ref, u_ref, *, aw):
    nb, tb, d = h_ref.shape
    rows = nb * tb
    hb = h_ref[...].reshape(rows, d).astype(BF16)
    cos = cos_ref[...]
    sin = sin_ref[...]
    lane = lax.broadcasted_iota(jnp.int32, (1, LANES), 1)
    first_half = (lane % HEAD_DIM) < (HEAD_DIM // 2)

    def rope(x):
        rot = jnp.where(first_half,
                        pltpu.roll(x, LANES - HEAD_DIM // 2, 1),
                        pltpu.roll(x, HEAD_DIM // 2, 1))
        return x * cos + rot * sin

    def put(ref, j, val):
        ref[:, j] = val.astype(BF16).reshape(nb, tb, LANES)

    q = _dot(hb, w_ref[:, 0:aw])
    k = _dot(hb, w_ref[:, aw:2 * aw])
    v = _dot(hb, w_ref[:, 2 * aw:3 * aw])
    scale = HEAD_DIM ** -0.5 * LOG2E
    for j in range(aw // LANES):
        sl = slice(j * LANES, (j + 1) * LANES)
        put(q_ref, j, rope(q[:, sl]) * scale)
        put(k_ref, j, rope(k[:, sl]))
        put(v_ref, j, v[:, sl])
    u_ref[...] = _dot(hb, w_ref[:, 3 * aw:]).astype(BF16).reshape(u_ref.shape)


def _proj(h, h_spec, w_in, l, cos_t, sin_t, aw, batch, n_tblk):
    d, n = w_in.shape[1:]
    sw = n - 3 * aw
    rows = batch * TBLK
    tab = pl.BlockSpec((rows, LANES), lambda i: (i, 0))
    pairs = aw // LANES
    qkv_spec = pl.BlockSpec((batch, pairs, TBLK, LANES), lambda i: (0, 0, i, 0))
    qkv_shape = jax.ShapeDtypeStruct((batch, pairs, n_tblk * TBLK, LANES), BF16)
    return pl.pallas_call(
        functools.partial(_proj_kernel, aw=aw),
        grid=(n_tblk,),
        in_specs=[h_spec, _layer_spec(w_in, l), tab, tab],
        out_specs=[qkv_spec] * 3 + [pl.BlockSpec((None, batch, TBLK, sw), lambda i: (i, 0, 0, 0))],
        out_shape=[qkv_shape] * 3 + [jax.ShapeDtypeStruct((n_tblk, batch, TBLK, sw), BF16)],
        compiler_params=pltpu.CompilerParams(vmem_limit_bytes=VMEM_LIMIT),
    )(h, w_in, cos_t, sin_t)


def _attn_kernel(q_ref, k_ref, v_ref, o_ref,
                 xf, x4, q12, k12, v12, q3, k3, v3, acc12, den12, mx12, acc3, den3, mx3, onat,
                 bias2, bias_c, s_all, p_all):
    seq = q_ref.shape[0]
    n4 = seq // N_DIL4
    pad_blk = QBLK + n4
    base2 = QBLK + seq

    head0 = lax.broadcasted_iota(jnp.int32, (1, LANES), 1) < HEAD_DIM
    ones_h0 = jnp.where(head0, 1.0, 0.0).astype(BF16)
    ones_h1 = jnp.where(head0, 0.0, 1.0).astype(BF16)

    qi = lax.broadcasted_iota(jnp.int32, (QBLK, 2 * QBLK), 0)
    kj = lax.broadcasted_iota(jnp.int32, (QBLK, 2 * QBLK), 1)
    band = (kj >= qi) & (kj <= qi + QBLK)
    bias2[0] = jnp.where(band, 0.0, NEG_INF).astype(F32)
    bias2[1] = jnp.where(band & (kj >= QBLK), 0.0, NEG_INF).astype(F32)
    qi = lax.broadcasted_iota(jnp.int32, (QBLK, QBLK), 0)
    kj = lax.broadcasted_iota(jnp.int32, (QBLK, QBLK), 1)
    bias_c[...] = jnp.where(kj <= qi, 0.0, NEG_INF).astype(F32)

    nat = [ref[...] for ref in (q_ref, k_ref, v_ref)]
    for i in range(3):
        xf[i] = nat[i].astype(F32)
    for r in range(N_DIL4):
        for i in range(3):
            x4[i, r * n4:(r + 1) * n4, :] = xf[i, pl.ds(r, n4, stride=N_DIL4), :]
    q12[0:seq, :] = nat[0]
    zero_blk = jnp.zeros((QBLK, LANES), BF16)
    for dst, i in ((k12, 1), (v12, 2)):
        dst[0:QBLK, :] = zero_blk
        dst[QBLK:base2, :] = nat[i]
        for r in range(N_DIL4):
            b = base2 + r * pad_blk
            dst[b:b + QBLK, :] = zero_blk
            dst[b + QBLK:b + pad_blk, :] = x4[i, r * n4:(r + 1) * n4, :].astype(BF16)
    for r in range(N_DIL4):
        q12[seq + r * n4:seq + (r + 1) * n4, :] = x4[0, r * n4:(r + 1) * n4, :].astype(BF16)
    for r16 in range(N_DIL16):
        rows = pl.ds((r16 % N_DIL4) * n4 + r16 // N_DIL4, QBLK, stride=N_DIL4)
        for dst, i in ((q3, 0), (k3, 1), (v3, 2)):
            dst[r16 * QBLK:(r16 + 1) * QBLK, :] = x4[i, rows, :].astype(BF16)

    def stage_a(s_buf, nk, qb, kb):
        zero = jnp.zeros_like(kb)
        kblk = jnp.concatenate([jnp.where(head0, kb, zero), jnp.where(head0, zero, kb)], axis=0)
        s_buf[:, 0:2 * nk] = _dot_nt(qb, kblk)

    def stage_b(s_buf, p_buf, nk, bias):
        ms = [jnp.max(s_buf[:, a * nk:(a + 1) * nk] + bias, axis=-1, keepdims=True)
              for a in range(2)]
        for a in range(2):
            shifted = (s_buf[:, a * nk:(a + 1) * nk] - ms[a]) + bias
            p_buf[:, a * nk:(a + 1) * nk] = jnp.exp2(shifted).astype(BF16)
        return jnp.where(head0, jnp.broadcast_to(ms[0], (QBLK, LANES)),
                         jnp.broadcast_to(ms[1], (QBLK, LANES)))

    def stage_c(p_buf, nk, vb):
        zero = jnp.zeros_like(vb)
        w = jnp.concatenate(
            [jnp.concatenate([jnp.where(head0, vb, zero), jnp.broadcast_to(ones_h0, vb.shape)], axis=1),
             jnp.concatenate([jnp.where(head0, zero, vb), jnp.broadcast_to(ones_h1, vb.shape)], axis=1)],
            axis=0)
        r = _dot(p_buf[:, 0:2 * nk], w)
        return r[:, :LANES], r[:, LANES:]

    def run_pipeline(n_units, a_fn, bias_fn, max_fn, c_fn, nk):
        def a(u):
            u = jnp.int32(u)
            a_fn(u, s_all.at[u % SLOTS])

        def b(u):
            u = jnp.int32(u)
            slot = u % SLOTS
            max_fn(u, stage_b(s_all.at[slot], p_all.at[slot], nk, bias_fn(u)))

        def c(u):
            u = jnp.int32(u)
            slot = u % SLOTS
            c_fn(u, p_all.at[slot])

        g = GROUP
        for u in range(2 * g):
            a(u)
        for u in range(g):
            b(u)

        def body(j, carry):
            u0 = g * j
            for i in range(g):
                c(u0 + i)
            for i in range(g):
                b(u0 + g + i)
            for i in range(g):
                a(u0 + 2 * g + i)
            return carry

        lax.fori_loop(0, n_units // g - 2, body, 0)
        for u in range(n_units - 2 * g, n_units - g):
            c(u)
        for u in range(n_units - g, n_units):
            b(u)
        for u in range(n_units - g, n_units):
            c(u)

    blocks1 = seq // QBLK
    blocks4 = n4 // QBLK

    def offsets12(u):
        v = u - blocks1
        r = v // blocks4
        blk = v % blocks4
        is_nat = u < blocks1
        q_off = jnp.where(is_nat, u * QBLK, seq + v * QBLK)
        k_off = jnp.where(is_nat, u * QBLK, base2 + r * pad_blk + blk * QBLK)
        first = jnp.where(is_nat, u == 0, blk == 0)
        return pl.multiple_of(q_off, QBLK), pl.multiple_of(k_off, QBLK), first.astype(jnp.int32)

    def a12(u, s_buf):
        q_off, k_off, _ = offsets12(u)
        stage_a(s_buf, 2 * QBLK, q12[pl.ds(q_off, QBLK), :], k12[pl.ds(k_off, 2 * QBLK), :])

    def bias12(u):
        return bias2[offsets12(u)[2]]

    def max12(u, m):
        mx12[pl.ds(offsets12(u)[0], QBLK), :] = m

    def c12(u, p_buf):
        q_off, k_off, _ = offsets12(u)
        acc, den = stage_c(p_buf, 2 * QBLK, v12[pl.ds(k_off, 2 * QBLK), :])
        acc12[pl.ds(q_off, QBLK), :] = acc
        den12[pl.ds(q_off, QBLK), :] = den

    run_pipeline(blocks1 + N_DIL4 * blocks4, a12, bias12, max12, c12, 2 * QBLK)

    def rows3(u):
        return pl.ds(pl.multiple_of(u * QBLK, QBLK), QBLK)

    def a3(u, s_buf):
        stage_a(s_buf, QBLK, q3[rows3(u), :], k3[rows3(u), :])

    def dst3(u):
        return pl.ds((u % N_DIL4) * n4 + u // N_DIL4, QBLK, stride=N_DIL4)

    def max3(u, m):
        mx3[dst3(u), :] = m

    def c3(u, p_buf):
        acc, den = stage_c(p_buf, QBLK, v3[rows3(u), :])
        acc3[dst3(u), :] = acc
        den3[dst3(u), :] = den

    run_pipeline(N_DIL16, a3, lambda u: bias_c[...], max3, c3, QBLK)

    def merge(idx, carry):
        r = idx // blocks4
        c = idx % blocks4
        rn = pl.ds(c * QBLK * N_DIL4 + r, QBLK, stride=N_DIL4)
        r3 = pl.ds(pl.multiple_of(r * n4 + c * QBLK, QBLK), QBLK)
        r2 = pl.ds(pl.multiple_of(seq + r * n4 + c * QBLK, QBLK), QBLK)
        ma, mb, mc = mx12[rn, :], mx12[r2, :], mx3[r3, :]
        m = jnp.maximum(jnp.maximum(ma, mb), mc)
        wa, wb, wc = jnp.exp2(ma - m), jnp.exp2(mb - m), jnp.exp2(mc - m)
        num = wa * acc12[rn, :] + wb * acc12[r2, :] + wc * acc3[r3, :]
        den = wa * den12[rn, :] + wb * den12[r2, :] + wc * den3[r3, :]
        onat[rn, :] = num / den
        return carry

    lax.fori_loop(0, N_DIL4 * blocks4, merge, 0)
    o_ref[...] = onat[...].astype(o_ref.dtype)


def _attention(q, k, v):
    batch, pairs, seq, _ = q.shape
    assert seq == N_DIL16 * QBLK and DILATION_PAIRS == ((128, 1), (512, 4), (2048, 16))
    blk = pl.BlockSpec((None, None, seq, LANES), lambda j: (j // pairs, j % pairs, 0, 0))
    n4 = seq // N_DIL4
    pad_rows = QBLK + seq + N_DIL4 * (QBLK + n4)
    vm = pltpu.VMEM
    scratch = [
        vm((3, seq, LANES), F32), vm((3, seq, LANES), F32),
        vm((2 * seq, LANES), BF16), vm((pad_rows, LANES), BF16), vm((pad_rows, LANES), BF16),
        vm((seq, LANES), BF16), vm((seq, LANES), BF16), vm((seq, LANES), BF16),
        vm((2 * seq, LANES), F32), vm((2 * seq, LANES), F32), vm((2 * seq, LANES), F32),
        vm((seq, LANES), F32), vm((seq, LANES), F32), vm((seq, LANES), F32),
        vm((seq, LANES), F32),
        vm((2, QBLK, 2 * QBLK), F32), vm((QBLK, QBLK), F32),
        vm((SLOTS, QBLK, 4 * QBLK), F32), vm((SLOTS, QBLK, 4 * QBLK), BF16),
    ]
    return pl.pallas_call(
        _attn_kernel,
        grid=(batch * pairs,),
        in_specs=[blk, blk, blk],
        out_specs=blk,
        out_shape=jax.ShapeDtypeStruct(q.shape, BF16),
        scratch_shapes=scratch,
        compiler_params=pltpu.CompilerParams(vmem_limit_bytes=VMEM_LIMIT),
    )(q, k, v)


def _ssm_kernel(u0_ref, un_ref, perm_ref, permt_ref, bh_ref, ch_ref, lam_ref, d_ref, wg_ref, bg_ref,
                gain_ref, o_ref, up0, up1, bu0, bu1, xs0, xs1, ys0, ys1, ya, gt, zb, st):
    batch, tb, sw = un_ref.shape
    half_states = bh_ref.shape[2] // 2
    half_ch = bh_ref.shape[1]
    cchunk = 256
    g = pl.program_id(0)

    chunks = [(h, c) for h in range(2) for c in range(0, half_states, cchunk)]

    def load_block(u_ref, up):
        up[...] = _dot(perm_ref[...], u_ref[...].reshape(batch * tb, sw)).astype(BF16)

    def expand_cols(up, bu, h, c):
        lhs = up[:, h * half_ch:(h + 1) * half_ch]
        for part in range(2):
            lo = part * half_states + c
            bu[:, h * 2 * half_states + lo:h * 2 * half_states + lo + cchunk] = _dot(
                lhs, bh_ref[h, :, lo:lo + cchunk])

    def scan_cols(bu, xs, h, c):
        for sub in range(c, c + cchunk, LANES):
            cre = slice(h * 2 * half_states + sub, h * 2 * half_states + sub + LANES)
            cim = slice(cre.start + half_states, cre.stop + half_states)
            cl = slice(h * half_states + sub, h * half_states + sub + LANES)
            lr = lam_ref[0:1, cl]
            li = lam_ref[1:2, cl]
            xr = st[:, cre]
            xi = st[:, cim]
            for i in range(tb):
                r = slice(i * batch, (i + 1) * batch)
                xr, xi = (lr * xr - li * xi + bu[r, cre], lr * xi + li * xr + bu[r, cim])
                xs[r, cre] = xr.astype(BF16)
                xs[r, cim] = xi.astype(BF16)
            st[:, cre] = xr
            st[:, cim] = xi

    def project_cols(xs, ys, h, c):
        out = slice(h * half_ch, (h + 1) * half_ch)
        acc = ys[:, out]
        for part in range(2):
            lo = part * half_states + c
            acc = acc + _dot(xs[:, h * 2 * half_states + lo:h * 2 * half_states + lo + cchunk],
                             ch_ref[h, lo:lo + cchunk, :])
        ys[:, out] = acc

    pieces = len(chunks) // 2
    piece_rows = batch * tb // pieces

    def act_rows(ys, i):
        r = slice(i * piece_rows, (i + 1) * piece_rows)
        ya[r, :] = jax.nn.gelu(ys[r, :])

    def gate_all():
        gt[...] = _dot(ya[...].astype(BF16), wg_ref[...]) + bg_ref[...]

    def norm_rows(i):
        r = slice(i * piece_rows, (i + 1) * piece_rows)
        z = ya[r, :] * (1.0 / (1.0 + jnp.exp(-gt[r, :])))
        z = z * lax.rsqrt(jnp.mean(z * z, axis=-1, keepdims=True) + RMS_EPS) * gain_ref[...]
        zb[r, :] = z.astype(BF16)

    def step(xs_q, ys_q, up_p, bu_p, xs_p, ys_p, up_q, bu_q):
        load_block(un_ref, up_q)
        for idx, (h, c) in enumerate(chunks):
            if idx < pieces:
                act_rows(ys_p, idx)
            else:
                if idx == pieces:
                    gate_all()
                    ys_p[...] = d_ref[...] * up_p[...].astype(F32)
                norm_rows(idx - pieces)
            project_cols(xs_q, ys_q, h, c)
            scan_cols(bu_p, xs_p, h, c)
            expand_cols(up_q, bu_q, h, c)
        o_ref[...] = _dot(permt_ref[...], zb[...]).astype(o_ref.dtype)

    @pl.when(g == 0)
    def _():
        st[...] = jnp.zeros_like(st)
        xs1[...] = jnp.zeros_like(xs1)
        ys0[...] = jnp.zeros_like(ys0)
        ys1[...] = jnp.zeros_like(ys1)
        load_block(u0_ref, up0)
        for h, c in chunks:
            expand_cols(up0, bu0, h, c)

    @pl.when(g % 2 == 0)
    def _():
        step(xs1, ys1, up0, bu0, xs0, ys0, up1, bu1)

    @pl.when(g % 2 == 1)
    def _():
        step(xs0, ys0, up1, bu1, xs1, ys1, up0, bu0)


def _ssm(u, perm, perm_t, l, bh, ch, lam, d_skip, w_glu, b_glu, gain):
    n_tblk, batch, tb, sw = u.shape
    assert n_tblk % 2 == 0
    m = batch * tb
    n_state = bh.shape[1] * bh.shape[3]
    full = lambda a: pl.BlockSpec(a.shape, lambda i: (0,) * a.ndim, pipeline_mode=pl.Buffered(1))
    consts = (bh, ch, lam, d_skip, w_glu, b_glu, gain)
    last = n_tblk - 1
    vm = pltpu.VMEM
    return pl.pallas_call(
        _ssm_kernel,
        grid=(n_tblk + 2,),
        in_specs=[pl.BlockSpec((None, batch, tb, sw), lambda i: (0, 0, 0, 0)),
                  pl.BlockSpec((None, batch, tb, sw), lambda i: (jnp.minimum(i + 1, last), 0, 0, 0)),
                  full(perm), full(perm_t)] + [_layer_spec(a, l, single_buffer=True) for a in consts],
        out_specs=pl.BlockSpec((m, sw), lambda i: (jnp.maximum(i - 2, 0), 0)),
        out_shape=jax.ShapeDtypeStruct((n_tblk * m, sw), BF16),
        scratch_shapes=[vm((m, sw), BF16), vm((m, sw), BF16),
                        vm((m, n_state), F32), vm((m, n_state), F32),
                        vm((m, n_state), BF16), vm((m, n_state), BF16),
                        vm((m, sw), F32), vm((m, sw), F32),
                        vm((m, sw), F32), vm((m, sw), F32), vm((m, sw), BF16),
                        vm((batch, n_state), F32)],
        compiler_params=pltpu.CompilerParams(dimension_semantics=("arbitrary",),
                                             vmem_limit_bytes=VMEM_LIMIT),
    )(u, u, perm, perm_t, *consts)


def _ssm_params(a_re, a_im, log_dt, b_re, b_im, c_re, c_im):
    depth, g, p = a_re.shape
    n = b_re.shape[-1]
    dt = jnp.exp(log_dt)[..., None]
    mag = jnp.exp(a_re * dt)
    ang = a_im * dt
    lb_re = mag * jnp.cos(ang)
    lb_im = mag * jnp.sin(ang)
    den = a_re * a_re + a_im * a_im
    nr = lb_re - 1.0
    ni = lb_im
    cr = (nr * a_re + ni * a_im) / den
    ci = (ni * a_re - nr * a_im) / den
    bb_re = cr[..., None] * b_re - ci[..., None] * b_im
    bb_im = cr[..., None] * b_im + ci[..., None] * b_re
    gh = g // 2
    eye = jnp.eye(gh, dtype=F32)
    halves = (slice(0, gh), slice(gh, g))

    def pack_b(bb):
        return jnp.einsum('lgpn,gh->lgnhp', bb, eye).reshape(depth, gh * n, gh * p)

    def pack_c(c):
        return jnp.einsum('lgnp,gh->lgphn', c, eye).reshape(depth, gh * p, gh * n)

    bh = jnp.stack([jnp.concatenate([pack_b(bb_re[:, s]), pack_b(bb_im[:, s])], axis=2)
                    for s in halves], axis=1)
    ch = jnp.stack([jnp.concatenate([pack_c(c_re[:, s]), -pack_c(c_im[:, s])], axis=1)
                    for s in halves], axis=1)
    lam = jnp.stack([lb_re.reshape(depth, -1), lb_im.reshape(depth, -1)], axis=1)
    return bh.astype(BF16), ch.astype(BF16), lam


def _layer_norm(x, g, b):
    mu = jnp.mean(x, axis=-1, keepdims=True)
    xc = x - mu
    var = jnp.mean(xc * xc, axis=-1, keepdims=True)
    return xc * lax.rsqrt(var + LN_EPS) * g + b


def _post_kernel(h_ref, a_ref, s_ref, ag_ref, wo_ref, bo_ref, g1_ref, b1_ref,
                 w1_ref, bf1_ref, w2_ref, bf2_ref, g2_ref, b2_ref, o_ref, *, alpha):
    nb, tb, d = h_ref.shape
    rows = nb * tb
    pairs = a_ref.shape[1]
    aw = pairs * LANES
    a = jnp.concatenate([a_ref[:, j].reshape(rows, LANES) for j in range(pairs)], axis=1).astype(F32)
    a = a * lax.rsqrt(jnp.mean(a * a, axis=-1, keepdims=True) + RMS_EPS) * ag_ref[...]
    mix = (_dot(a.astype(BF16), wo_ref[0:aw, :]) + _dot(s_ref[...], wo_ref[aw:, :]) + bo_ref[...])
    h1 = _layer_norm(alpha * h_ref[...].reshape(rows, d) + mix, g1_ref[...], b1_ref[...])
    h1b = h1.astype(BF16)
    ff = jnp.zeros_like(h1)
    for c in range(0, w1_ref.shape[1], FF_CHUNK):
        act = jnp.maximum(_dot(h1b, w1_ref[:, c:c + FF_CHUNK]) + bf1_ref[:, c:c + FF_CHUNK], 0.0)
        ff = ff + _dot((act * act).astype(BF16), w2_ref[c:c + FF_CHUNK, :])
    out = _layer_norm(alpha * h1 + ff + bf2_ref[...], g2_ref[...], b2_ref[...])
    o_ref[...] = out.reshape(nb, tb, d)


def _post(h, h_spec, attn, ssm, l, consts, alpha, out_shape, out_spec):
    batch, pairs, seq, _ = attn.shape
    n_tblk = seq // TBLK
    rows = batch * TBLK
    return pl.pallas_call(
        functools.partial(_post_kernel, alpha=alpha),
        grid=(n_tblk,),
        in_specs=[h_spec, pl.BlockSpec((batch, pairs, TBLK, LANES), lambda i: (0, 0, i, 0)),
                  pl.BlockSpec((rows, ssm.shape[1]), lambda i: (i, 0))]
                 + [_layer_spec(a, l, single_buffer=True) for a in consts],
        out_specs=out_spec,
        out_shape=out_shape,
        compiler_params=pltpu.CompilerParams(vmem_limit_bytes=VMEM_LIMIT),
    )(h, attn, ssm, *consts)


def kernel(x, positions, w_in, attn_gain, ssm_gain, ssm_a_re, ssm_a_im, ssm_log_dt, ssm_b_re,
           ssm_b_im, ssm_c_re, ssm_c_im, ssm_d, w_glu, b_glu, w_out, b_out, ln1_g, ln1_b,
           w_ff1, b_ff1, w_ff2, b_ff2, ln2_g, ln2_b):
    batch, seq, d = x.shape
    depth = w_in.shape[0]
    aw = attn_gain.shape[1]
    t = batch * seq
    n_tblk = seq // TBLK
    rows = batch * TBLK
    alpha = (2.0 * depth) ** 0.25
    vec = lambda a: a.reshape(depth, 1, -1).astype(F32)

    half = HEAD_DIM // 2
    inv_freq = ROPE_THETA ** (-jnp.arange(half, dtype=F32) * 2.0 / HEAD_DIM)
    freq_row = jnp.tile(inv_freq, LANES // half).reshape(1, LANES)
    sign_row = jnp.tile(jnp.concatenate([-jnp.ones(half, F32), jnp.ones(half, F32)]),
                        LANES // HEAD_DIM).reshape(1, LANES)
    pos_col = positions.reshape(batch, n_tblk, TBLK).transpose(1, 0, 2).reshape(t, 1).astype(F32)
    cos_t, sin_t = _rope_tables(pos_col, freq_row, sign_row)

    r_out = jnp.arange(rows, dtype=jnp.int32)
    r_in = (r_out % batch) * TBLK + r_out // batch
    perm = (r_in[:, None] == jnp.arange(rows, dtype=jnp.int32)[None, :]).astype(BF16)
    perm_t = perm.T

    w_in_b = w_in.astype(BF16)
    bh, ch, lam = _ssm_params(ssm_a_re, ssm_a_im, ssm_log_dt, ssm_b_re, ssm_b_im, ssm_c_re, ssm_c_im)
    ssm_consts = (bh, ch, lam, vec(ssm_d), w_glu.astype(BF16), vec(b_glu), vec(ssm_gain))
    post_consts = (vec(attn_gain), w_out.astype(BF16), vec(b_out), vec(ln1_g), vec(ln1_b),
                   w_ff1.astype(BF16), vec(b_ff1), w_ff2.astype(BF16), vec(b_ff2), vec(ln2_g),
                   vec(ln2_b))

    by_seq_spec = pl.BlockSpec((batch, None, TBLK, d), lambda i: (0, i, 0, 0))
    by_blk_spec = pl.BlockSpec((batch, TBLK, d), lambda i: (i, 0, 0))
    h = x.reshape(batch, n_tblk, TBLK, d)
    h_spec = by_seq_spec
    for l in range(depth):
        q, k, v, u = _proj(h, h_spec, w_in_b, l, cos_t, sin_t, aw, batch, n_tblk)
        attn = _attention(q, k, v)
        ssm = _ssm(u, perm, perm_t, l, *ssm_consts)
        if l == depth - 1:
            out_shape, out_spec = jax.ShapeDtypeStruct((batch, n_tblk, TBLK, d), F32), by_seq_spec
        else:
            out_shape, out_spec = jax.ShapeDtypeStruct((n_tblk * batch, TBLK, d), F32), by_blk_spec
        h = _post(h, h_spec, attn, ssm, l, post_consts, alpha, out_shape, out_spec)
        h_spec = by_blk_spec
    return h.reshape(batch, seq, d)
```

```python
import functools
import math

import jax
import jax.numpy as jnp
from jax import lax
from jax.experimental import pallas as pl
from jax.experimental.pallas import tpu as pltpu

F32 = jnp.float32
BF16 = jnp.bfloat16

HEAD_DIM = 64
LANES = 128
DILATION_PAIRS = ((128, 1), (512, 4), (2048, 16))
QBLK = 128
N_DIL4 = 4
N_DIL16 = 16
GROUP = 8
SLOTS = 2 * GROUP
ROPE_THETA = 10000.0
LN_EPS = 1e-5
RMS_EPS = 1e-6
NEG_INF = -1e30
LOG2E = math.log2(math.e)
VMEM_LIMIT = 56 * 1024 * 1024

TBLK = 16
FF_CHUNK = 1024


def _dot(a, b):
    return jnp.dot(a, b, preferred_element_type=F32)


def _dot_nt(a, b):
    return lax.dot_general(a, b, (((1,), (1,)), ((), ())), preferred_element_type=F32)


def _layer_spec(arr, l, single_buffer=False):
    mode = dict(pipeline_mode=pl.Buffered(1)) if single_buffer else {}
    return pl.BlockSpec((None,) + arr.shape[1:], lambda i: (l,) + (0,) * (arr.ndim - 1), **mode)


def _rope_table_kernel(pos_ref, freq_ref, sign_ref, cos_ref, sin_ref):
    ang = pos_ref[...] * freq_ref[...]
    cos_ref[...] = jnp.cos(ang)
    sin_ref[...] = jnp.sin(ang) * sign_ref[...]


def _rope_tables(pos_col, freq_row, sign_row):
    t = pos_col.shape[0]
    tm = 2048
    row = pl.BlockSpec((1, LANES), lambda i: (0, 0))
    tab = pl.BlockSpec((tm, LANES), lambda i: (i, 0))
    return pl.pallas_call(
        _rope_table_kernel,
        grid=(t // tm,),
        in_specs=[pl.BlockSpec((tm, 1), lambda i: (i, 0)), row, row],
        out_specs=[tab, tab],
        out_shape=[jax.ShapeDtypeStruct((t, LANES), F32)] * 2,
    )(pos_col, freq_row, sign_row)


def _proj_kernel(h_ref, w_ref, cos_ref, sin_ref, q_ref, k_ref, v_ref, u_ref, *, aw):
    nb, tb, d = h_ref.shape
    rows = nb * tb
    hb = h_ref[...].reshape(rows, d).astype(BF16)
    cos = cos_ref[...]
    sin = sin_ref[...]
    lane = lax.broadcasted_iota(jnp.int32, (1, LANES), 1)
    first_half = (lane % HEAD_DIM) < (HEAD_DIM // 2)

    def rope(x):
        rot = jnp.where(first_half,
                        pltpu.roll(x, LANES - HEAD_DIM // 2, 1),
                        pltpu.roll(x, HEAD_DIM // 2, 1))
        return x * cos + rot * sin

    def put(ref, j, val):
        ref[:, j] = val.reshape(nb, tb, LANES)

    q = _dot(hb, w_ref[:, 0:aw])
    k = _dot(hb, w_ref[:, aw:2 * aw])
    v = _dot(hb, w_ref[:, 2 * aw:3 * aw])
    scale = HEAD_DIM ** -0.5 * LOG2E
    for j in range(aw // LANES):
        sl = slice(j * LANES, (j + 1) * LANES)
        put(q_ref, j, rope(q[:, sl]) * scale)
        put(k_ref, j, rope(k[:, sl]))
        put(v_ref, j, v[:, sl])
    u_ref[...] = _dot(hb, w_ref[:, 3 * aw:]).astype(BF16).reshape(u_ref.shape)


def _proj(h, h_spec, w_in, l, cos_t, sin_t, aw, batch, n_tblk):
    d, n = w_in.shape[1:]
    sw = n - 3 * aw
    rows = batch * TBLK
    tab = pl.BlockSpec((rows, LANES), lambda i: (i, 0))
    pairs = aw // LANES
    qkv_spec = pl.BlockSpec((batch, pairs, TBLK, LANES), lambda i: (0, 0, i, 0))
    qkv_shape = jax.ShapeDtypeStruct((batch, pairs, n_tblk * TBLK, LANES), F32)
    return pl.pallas_call(
        functools.partial(_proj_kernel, aw=aw),
        grid=(n_tblk,),
        in_specs=[h_spec, _layer_spec(w_in, l), tab, tab],
        out_specs=[qkv_spec] * 3 + [pl.BlockSpec((None, batch, TBLK, sw), lambda i: (i, 0, 0, 0))],
        out_shape=[qkv_shape] * 3 + [jax.ShapeDtypeStruct((n_tblk, batch, TBLK, sw), BF16)],
        compiler_params=pltpu.CompilerParams(vmem_limit_bytes=VMEM_LIMIT),
    )(h, w_in, cos_t, sin_t)


def _attn_kernel(q_ref, k_ref, v_ref, o_ref,
                 x4, q12, k12, v12, q3, k3, v3, acc12, den12, mx12, acc3, den3, mx3, onat,
                 bias2, bias_c, s_all, p_all):
    seq = q_ref.shape[0]
    n4 = seq // N_DIL4
    pad_blk = QBLK + n4
    base2 = QBLK + seq

    head0 = lax.broadcasted_iota(jnp.int32, (1, LANES), 1) < HEAD_DIM
    ones_h0 = jnp.where(head0, 1.0, 0.0).astype(BF16)
    ones_h1 = jnp.where(head0, 0.0, 1.0).astype(BF16)

    qi = lax.broadcasted_iota(jnp.int32, (QBLK, 2 * QBLK), 0)
    kj = lax.broadcasted_iota(jnp.int32, (QBLK, 2 * QBLK), 1)
    band = (kj >= qi) & (kj <= qi + QBLK)
    bias2[0] = jnp.where(band, 0.0, NEG_INF).astype(F32)
    bias2[1] = jnp.where(band & (kj >= QBLK), 0.0, NEG_INF).astype(F32)
    qi = lax.broadcasted_iota(jnp.int32, (QBLK, QBLK), 0)
    kj = lax.broadcasted_iota(jnp.int32, (QBLK, QBLK), 1)
    bias_c[...] = jnp.where(kj <= qi, 0.0, NEG_INF).astype(F32)

    srcs = (q_ref, k_ref, v_ref)
    nat = [ref[...].astype(BF16) for ref in srcs]
    for r in range(N_DIL4):
        for i in range(3):
            x4[i, r * n4:(r + 1) * n4, :] = srcs[i][pl.ds(r, n4, stride=N_DIL4), :]
    q12[0:seq, :] = nat[0]
    zero_blk = jnp.zeros((QBLK, LANES), BF16)
    for dst, i in ((k12, 1), (v12, 2)):
        dst[0:QBLK, :] = zero_blk
        dst[QBLK:base2, :] = nat[i]
        for r in range(N_DIL4):
            b = base2 + r * pad_blk
            dst[b:b + QBLK, :] = zero_blk
            dst[b + QBLK:b + pad_blk, :] = x4[i, r * n4:(r + 1) * n4, :].astype(BF16)
    for r in range(N_DIL4):
        q12[seq + r * n4:seq + (r + 1) * n4, :] = x4[0, r * n4:(r + 1) * n4, :].astype(BF16)
    for r16 in range(N_DIL16):
        rows = pl.ds((r16 % N_DIL4) * n4 + r16 // N_DIL4, QBLK, stride=N_DIL4)
        for dst, i in ((q3, 0), (k3, 1), (v3, 2)):
            dst[r16 * QBLK:(r16 + 1) * QBLK, :] = x4[i, rows, :].astype(BF16)

    def stage_a(s_buf, nk, qb, kb):
        zero = jnp.zeros_like(kb)
        kblk = jnp.concatenate([jnp.where(head0, kb, zero), jnp.where(head0, zero, kb)], axis=0)
        s_buf[:, 0:2 * nk] = _dot_nt(qb, kblk)

    def stage_b(s_buf, p_buf, nk, bias):
        ms = [jnp.max(s_buf[:, a * nk:(a + 1) * nk] + bias, axis=-1, keepdims=True)
              for a in range(2)]
        for a in range(2):
            shifted = (s_buf[:, a * nk:(a + 1) * nk] - ms[a]) + bias
            p_buf[:, a * nk:(a + 1) * nk] = jnp.exp2(shifted).astype(BF16)
        return jnp.where(head0, jnp.broadcast_to(ms[0], (QBLK, LANES)),
                         jnp.broadcast_to(ms[1], (QBLK, LANES)))

    def stage_c(p_buf, nk, vb):
        zero = jnp.zeros_like(vb)
        w = jnp.concatenate(
            [jnp.concatenate([jnp.where(head0, vb, zero), jnp.broadcast_to(ones_h0, vb.shape)], axis=1),
             jnp.concatenate([jnp.where(head0, zero, vb), jnp.broadcast_to(ones_h1, vb.shape)], axis=1)],
            axis=0)
        r = _dot(p_buf[:, 0:2 * nk], w)
        return r[:, :LANES], r[:, LANES:]

    def run_pipeline(n_units, a_fn, bias_fn, max_fn, c_fn, nk):
        def a(u):
            u = jnp.int32(u)
            a_fn(u, s_all.at[u % SLOTS])

        def b(u):
            u = jnp.int32(u)
            slot = u % SLOTS
            max_fn(u, stage_b(s_all.at[slot], p_all.at[slot], nk, bias_fn(u)))

        def c(u):
            u = jnp.int32(u)
            slot = u % SLOTS
            c_fn(u, p_all.at[slot])

        g = GROUP
        for u in range(2 * g):
            a(u)
        for u in range(g):
            b(u)

        def body(j, carry):
            u0 = g * j
            for i in range(g):
                c(u0 + i)
            for i in range(g):
                b(u0 + g + i)
            for i in range(g):
                a(u0 + 2 * g + i)
            return carry

        lax.fori_loop(0, n_units // g - 2, body, 0)
        for u in range(n_units - 2 * g, n_units - g):
            c(u)
        for u in range(n_units - g, n_units):
            b(u)
        for u in range(n_units - g, n_units):
            c(u)

    blocks1 = seq // QBLK
    blocks4 = n4 // QBLK

    def offsets12(u):
        v = u - blocks1
        r = v // blocks4
        blk = v % blocks4
        is_nat = u < blocks1
        q_off = jnp.where(is_nat, u * QBLK, seq + v * QBLK)
        k_off = jnp.where(is_nat, u * QBLK, base2 + r * pad_blk + blk * QBLK)
        first = jnp.where(is_nat, u == 0, blk == 0)
        return pl.multiple_of(q_off, QBLK), pl.multiple_of(k_off, QBLK), first.astype(jnp.int32)

    def a12(u, s_buf):
        q_off, k_off, _ = offsets12(u)
        stage_a(s_buf, 2 * QBLK, q12[pl.ds(q_off, QBLK), :], k12[pl.ds(k_off, 2 * QBLK), :])

    def bias12(u):
        return bias2[offsets12(u)[2]]

    def max12(u, m):
        mx12[pl.ds(offsets12(u)[0], QBLK), :] = m

    def c12(u, p_buf):
        q_off, k_off, _ = offsets12(u)
        acc, den = stage_c(p_buf, 2 * QBLK, v12[pl.ds(k_off, 2 * QBLK), :])
        acc12[pl.ds(q_off, QBLK), :] = acc
        den12[pl.ds(q_off, QBLK), :] = den

    run_pipeline(blocks1 + N_DIL4 * blocks4, a12, bias12, max12, c12, 2 * QBLK)

    def rows3(u):
        return pl.ds(pl.multiple_of(u * QBLK, QBLK), QBLK)

    def a3(u, s_buf):
        stage_a(s_buf, QBLK, q3[rows3(u), :], k3[rows3(u), :])

    def dst3(u):
        return pl.ds((u % N_DIL4) * n4 + u // N_DIL4, QBLK, stride=N_DIL4)

    def max3(u, m):
        mx3[dst3(u), :] = m

    def c3(u, p_buf):
        acc, den = stage_c(p_buf, QBLK, v3[rows3(u), :])
        acc3[dst3(u), :] = acc
        den3[dst3(u), :] = den

    run_pipeline(N_DIL16, a3, lambda u: bias_c[...], max3, c3, QBLK)

    def merge(idx, carry):
        r = idx // blocks4
        c = idx % blocks4
        rn = pl.ds(c * QBLK * N_DIL4 + r, QBLK, stride=N_DIL4)
        r3 = pl.ds(pl.multiple_of(r * n4 + c * QBLK, QBLK), QBLK)
        r2 = pl.ds(pl.multiple_of(seq + r * n4 + c * QBLK, QBLK), QBLK)
        ma, mb, mc = mx12[rn, :], mx12[r2, :], mx3[r3, :]
        m = jnp.maximum(jnp.maximum(ma, mb), mc)
        wa, wb, wc = jnp.exp2(ma - m), jnp.exp2(mb - m), jnp.exp2(mc - m)
        num = wa * acc12[rn, :] + wb * acc12[r2, :] + wc * acc3[r3, :]
        den = wa * den12[rn, :] + wb * den12[r2, :] + wc * den3[r3, :]
        onat[rn, :] = num / den
        return carry

    lax.fori_loop(0, N_DIL4 * blocks4, merge, 0)
    o_ref[...] = onat[...].astype(o_ref.dtype)


def _attention(q, k, v):
    batch, pairs, seq, _ = q.shape
    assert seq == N_DIL16 * QBLK and DILATION_PAIRS == ((128, 1), (512, 4), (2048, 16))
    blk = pl.BlockSpec((None, None, seq, LANES), lambda j: (j // pairs, j % pairs, 0, 0))
    n4 = seq // N_DIL4
    pad_rows = QBLK + seq + N_DIL4 * (QBLK + n4)
    vm = pltpu.VMEM
    scratch = [
        vm((3, seq, LANES), F32),
        vm((2 * seq, LANES), BF16), vm((pad_rows, LANES), BF16), vm((pad_rows, LANES), BF16),
        vm((seq, LANES), BF16), vm((seq, LANES), BF16), vm((seq, LANES), BF16),
        vm((2 * seq, LANES), F32), vm((2 * seq, LANES), F32), vm((2 * seq, LANES), F32),
        vm((seq, LANES), F32), vm((seq, LANES), F32), vm((seq, LANES), F32),
        vm((seq, LANES), F32),
        vm((2, QBLK, 2 * QBLK), F32), vm((QBLK, QBLK), F32),
        vm((SLOTS, QBLK, 4 * QBLK), F32), vm((SLOTS, QBLK, 4 * QBLK), BF16),
    ]
    return pl.pallas_call(
        _attn_kernel,
        grid=(batch * pairs,),
        in_specs=[blk, blk, blk],
        out_specs=blk,
        out_shape=jax.ShapeDtypeStruct(q.shape, BF16),
        scratch_shapes=scratch,
        compiler_params=pltpu.CompilerParams(vmem_limit_bytes=VMEM_LIMIT),
    )(q, k, v)


def _ssm_kernel(u0_ref, un_ref, perm_ref, permt_ref, bh_ref, ch_ref, lam_ref, d_ref, wg_ref, bg_ref,
                gain_ref, o_ref, up0, up1, bu0, bu1, xs0, xs1, ys0, ys1, ya, gt, zb, st):
    batch, tb, sw = un_ref.shape
    half_states = bh_ref.shape[2] // 2
    half_ch = bh_ref.shape[1]
    cchunk = 256
    g = pl.program_id(0)

    chunks = [(h, c) for h in range(2) for c in range(0, half_states, cchunk)]

    def load_block(u_ref, up):
        up[...] = _dot(perm_ref[...], u_ref[...].reshape(batch * tb, sw)).astype(BF16)

    def expand_cols(up, bu, h, c):
        lhs = up[:, h * half_ch:(h + 1) * half_ch]
        for part in range(2):
            lo = part * half_states + c
            bu[:, h * 2 * half_states + lo:h * 2 * half_states + lo + cchunk] = _dot(
                lhs, bh_ref[h, :, lo:lo + cchunk])

    def scan_cols(bu, xs, h, c):
        for sub in range(c, c + cchunk, LANES):
            cre = slice(h * 2 * half_states + sub, h * 2 * half_states + sub + LANES)
            cim = slice(cre.start + half_states, cre.stop + half_states)
            cl = slice(h * half_states + sub, h * half_states + sub + LANES)
            lr = lam_ref[0:1, cl]
            li = lam_ref[1:2, cl]
            xr = st[:, cre]
            xi = st[:, cim]
            for i in range(tb):
                r = slice(i * batch, (i + 1) * batch)
                xr, xi = (lr * xr - li * xi + bu[r, cre], lr * xi + li * xr + bu[r, cim])
                xs[r, cre] = xr.astype(BF16)
                xs[r, cim] = xi.astype(BF16)
            st[:, cre] = xr
            st[:, cim] = xi

    def project_half(xs, ys, h):
        out = slice(h * half_ch, (h + 1) * half_ch)
        ys[:, out] = ys[:, out] + _dot(xs[:, h * 2 * half_states:(h + 1) * 2 * half_states],
                                       ch_ref[h])

    pieces = len(chunks) // 2
    piece_rows = batch * tb // pieces

    def act_rows(ys, i):
        r = slice(i * piece_rows, (i + 1) * piece_rows)
        ya[r, :] = jax.nn.gelu(ys[r, :])

    def gate_all():
        gt[...] = _dot(ya[...].astype(BF16), wg_ref[...]) + bg_ref[...]

    def norm_rows(i):
        r = slice(i * piece_rows, (i + 1) * piece_rows)
        z = ya[r, :] * (1.0 / (1.0 + jnp.exp(-gt[r, :])))
        z = z * lax.rsqrt(jnp.mean(z * z, axis=-1, keepdims=True) + RMS_EPS) * gain_ref[...]
        zb[r, :] = z.astype(BF16)

    def step(xs_q, ys_q, up_p, bu_p, xs_p, ys_p, up_q, bu_q):
        load_block(un_ref, up_q)
        for idx, (h, c) in enumerate(chunks):
            if idx < pieces:
                act_rows(ys_p, idx)
            else:
                if idx == pieces:
                    gate_all()
                    ys_p[...] = d_ref[...] * up_p[...].astype(F32)
                norm_rows(idx - pieces)
            if c == 0:
                project_half(xs_q, ys_q, h)
            scan_cols(bu_p, xs_p, h, c)
            expand_cols(up_q, bu_q, h, c)
        o_ref[...] = _dot(permt_ref[...], zb[...]).astype(o_ref.dtype)

    @pl.when(g == 0)
    def _():
        st[...] = jnp.zeros_like(st)
        xs1[...] = jnp.zeros_like(xs1)
        ys0[...] = jnp.zeros_like(ys0)
        ys1[...] = jnp.zeros_like(ys1)
        load_block(u0_ref, up0)
        for h, c in chunks:
            expand_cols(up0, bu0, h, c)

    @pl.when(g % 2 == 0)
    def _():
        step(xs1, ys1, up0, bu0, xs0, ys0, up1, bu1)

    @pl.when(g % 2 == 1)
    def _():
        step(xs0, ys0, up1, bu1, xs1, ys1, up0, bu0)


def _ssm(u, perm, perm_t, l, bh, ch, lam, d_skip, w_glu, b_glu, gain):
    n_tblk, batch, tb, sw = u.shape
    assert n_tblk % 2 == 0
    m = batch * tb
    n_state = bh.shape[1] * bh.shape[3]
    full = lambda a: pl.BlockSpec(a.shape, lambda i: (0,) * a.ndim, pipeline_mode=pl.Buffered(1))
    consts = (bh, ch, lam, d_skip, w_glu, b_glu, gain)
    last = n_tblk - 1
    vm = pltpu.VMEM
    return pl.pallas_call(
        _ssm_kernel,
        grid=(n_tblk + 2,),
        in_specs=[pl.BlockSpec((None, batch, tb, sw), lambda i: (0, 0, 0, 0)),
                  pl.BlockSpec((None, batch, tb, sw), lambda i: (jnp.minimum(i + 1, last), 0, 0, 0)),
                  full(perm), full(perm_t)] + [_layer_spec(a, l, single_buffer=True) for a in consts],
        out_specs=pl.BlockSpec((m, sw), lambda i: (jnp.maximum(i - 2, 0), 0)),
        out_shape=jax.ShapeDtypeStruct((n_tblk * m, sw), BF16),
        scratch_shapes=[vm((m, sw), BF16), vm((m, sw), BF16),
                        vm((m, n_state), F32), vm((m, n_state), F32),
                        vm((m, n_state), BF16), vm((m, n_state), BF16),
                        vm((m, sw), F32), vm((m, sw), F32),
                        vm((m, sw), F32), vm((m, sw), F32), vm((m, sw), BF16),
                        vm((batch, n_state), F32)],
        compiler_params=pltpu.CompilerParams(dimension_semantics=("arbitrary",),
                                             vmem_limit_bytes=VMEM_LIMIT),
    )(u, u, perm, perm_t, *consts)


def _ssm_params(a_re, a_im, log_dt, b_re, b_im, c_re, c_im):
    depth, g, p = a_re.shape
    n = b_re.shape[-1]
    dt = jnp.exp(log_dt)[..., None]
    mag = jnp.exp(a_re * dt)
    ang = a_im * dt
    lb_re = mag * jnp.cos(ang)
    lb_im = mag * jnp.sin(ang)
    den = a_re * a_re + a_im * a_im
    nr = lb_re - 1.0
    ni = lb_im
    cr = (nr * a_re + ni * a_im) / den
    ci = (ni * a_re - nr * a_im) / den
    bb_re = cr[..., None] * b_re - ci[..., None] * b_im
    bb_im = cr[..., None] * b_im + ci[..., None] * b_re
    gh = g // 2
    eye = jnp.eye(gh, dtype=F32)
    halves = (slice(0, gh), slice(gh, g))

    def pack_b(bb):
        return jnp.einsum('lgpn,gh->lgnhp', bb, eye).reshape(depth, gh * n, gh * p)

    def pack_c(c):
        return jnp.einsum('lgnp,gh->lgphn', c, eye).reshape(depth, gh * p, gh * n)

    bh = jnp.stack([jnp.concatenate([pack_b(bb_re[:, s]), pack_b(bb_im[:, s])], axis=2)
                    for s in halves], axis=1)
    ch = jnp.stack([jnp.concatenate([pack_c(c_re[:, s]), -pack_c(c_im[:, s])], axis=1)
                    for s in halves], axis=1)
    lam = jnp.stack([lb_re.reshape(depth, -1), lb_im.reshape(depth, -1)], axis=1)
    return bh.astype(BF16), ch.astype(BF16), lam


def _layer_norm(x, g, b):
    mu = jnp.mean(x, axis=-1, keepdims=True)
    xc = x - mu
    var = jnp.mean(xc * xc, axis=-1, keepdims=True)
    return xc * lax.rsqrt(var + LN_EPS) * g + b


def _post_kernel(h_ref, a_ref, s_ref, ag_ref, wo_ref, bo_ref, g1_ref, b1_ref,
                 w1_ref, bf1_ref, w2_ref, bf2_ref, g2_ref, b2_ref, o_ref, *, alpha):
    nb, tb, d = h_ref.shape
    rows = nb * tb
    pairs = a_ref.shape[1]
    aw = pairs * LANES
    a = jnp.concatenate([a_ref[:, j].reshape(rows, LANES) for j in range(pairs)], axis=1).astype(F32)
    a = a * lax.rsqrt(jnp.mean(a * a, axis=-1, keepdims=True) + RMS_EPS) * ag_ref[...]
    mix = (_dot(a.astype(BF16), wo_ref[0:aw, :]) + _dot(s_ref[...], wo_ref[aw:, :]) + bo_ref[...])
    h1 = _layer_norm(alpha * h_ref[...].reshape(rows, d) + mix, g1_ref[...], b1_ref[...])
    h1b = h1.astype(BF16)
    ff = jnp.zeros_like(h1)
    for c in range(0, w1_ref.shape[1], FF_CHUNK):
        act = jnp.maximum(_dot(h1b, w1_ref[:, c:c + FF_CHUNK]) + bf1_ref[:, c:c + FF_CHUNK], 0.0)
        ff = ff + _dot((act * act).astype(BF16), w2_ref[c:c + FF_CHUNK, :])
    out = _layer_norm(alpha * h1 + ff + bf2_ref[...], g2_ref[...], b2_ref[...])
    o_ref[...] = out.reshape(nb, tb, d)


def _post(h, h_spec, attn, ssm, l, consts, alpha, out_shape, out_spec):
    batch, pairs, seq, _ = attn.shape
    n_tblk = seq // TBLK
    rows = batch * TBLK
    return pl.pallas_call(
        functools.partial(_post_kernel, alpha=alpha),
        grid=(n_tblk,),
        in_specs=[h_spec, pl.BlockSpec((batch, pairs, TBLK, LANES), lambda i: (0, 0, i, 0)),
                  pl.BlockSpec((rows, ssm.shape[1]), lambda i: (i, 0))]
                 + [_layer_spec(a, l, single_buffer=True) for a in consts],
        out_specs=out_spec,
        out_shape=out_shape,
        compiler_params=pltpu.CompilerParams(vmem_limit_bytes=VMEM_LIMIT),
    )(h, attn, ssm, *consts)


def kernel(x, positions, w_in, attn_gain, ssm_gain, ssm_a_re, ssm_a_im, ssm_log_dt, ssm_b_re,
           ssm_b_im, ssm_c_re, ssm_c_im, ssm_d, w_glu, b_glu, w_out, b_out, ln1_g, ln1_b,
           w_ff1, b_ff1, w_ff2, b_ff2, ln2_g, ln2_b):
    batch, seq, d = x.shape
    depth = w_in.shape[0]
    aw = attn_gain.shape[1]
    t = batch * seq
    n_tblk = seq // TBLK
    rows = batch * TBLK
    alpha = (2.0 * depth) ** 0.25
    vec = lambda a: a.reshape(depth, 1, -1).astype(F32)

    half = HEAD_DIM // 2
    inv_freq = ROPE_THETA ** (-jnp.arange(half, dtype=F32) * 2.0 / HEAD_DIM)
    freq_row = jnp.tile(inv_freq, LANES // half).reshape(1, LANES)
    sign_row = jnp.tile(jnp.concatenate([-jnp.ones(half, F32), jnp.ones(half, F32)]),
                        LANES // HEAD_DIM).reshape(1, LANES)
    pos_col = positions.reshape(batch, n_tblk, TBLK).transpose(1, 0, 2).reshape(t, 1).astype(F32)
    cos_t, sin_t = _rope_tables(pos_col, freq_row, sign_row)

    r_out = jnp.arange(rows, dtype=jnp.int32)
    r_in = (r_out % batch) * TBLK + r_out // batch
    perm = (r_in[:, None] == jnp.arange(rows, dtype=jnp.int32)[None, :]).astype(BF16)
    perm_t = perm.T

    w_in_b = w_in.astype(BF16)
    bh, ch, lam = _ssm_params(ssm_a_re, ssm_a_im, ssm_log_dt, ssm_b_re, ssm_b_im, ssm_c_re, ssm_c_im)
    ssm_consts = (bh, ch, lam, vec(ssm_d), w_glu.astype(BF16), vec(b_glu), vec(ssm_gain))
    post_consts = (vec(attn_gain), w_out.astype(BF16), vec(b_out), vec(ln1_g), vec(ln1_b),
                   w_ff1.astype(BF16), vec(b_ff1), w_ff2.astype(BF16), vec(b_ff2), vec(ln2_g),
                   vec(ln2_b))

    by_seq_spec = pl.BlockSpec((batch, None, TBLK, d), lambda i: (0, i, 0, 0))
    by_blk_spec = pl.BlockSpec((batch, TBLK, d), lambda i: (i, 0, 0))
    h = x.reshape(batch, n_tblk, TBLK, d)
    h_spec = by_seq_spec
    for l in range(depth):
        q, k, v, u = _proj(h, h_spec, w_in_b, l, cos_t, sin_t, aw, batch, n_tblk)
        attn = _attention(q, k, v)
        ssm = _ssm(u, perm, perm_t, l, *ssm_consts)
        if l == depth - 1:
            out_shape, out_spec = jax.ShapeDtypeStruct((batch, n_tblk, TBLK, d), F32), by_seq_spec
        else:
            out_shape, out_spec = jax.ShapeDtypeStruct((n_tblk * batch, TBLK, d), F32), by_blk_spec
        h = _post(h, h_spec, attn, ssm, l, post_consts, alpha, out_shape, out_spec)
        h_spec = by_blk_spec
    return h.reshape(batch, seq, d)
```

```python
import functools
import math

import jax
import jax.numpy as jnp
from jax import lax
from jax.experimental import pallas as pl
from jax.experimental.pallas import tpu as pltpu

F32 = jnp.float32
BF16 = jnp.bfloat16

HEAD_DIM = 64
LANES = 128
DILATION_PAIRS = ((128, 1), (512, 4), (2048, 16))
QBLK = 128
N_DIL4 = 4
N_DIL16 = 16
GROUP = 4
SLOTS = 2 * GROUP
ROPE_THETA = 10000.0
LN_EPS = 1e-5
RMS_EPS = 1e-6
NEG_INF = -1e30
LOG2E = math.log2(math.e)
VMEM_LIMIT = 56 * 1024 * 1024

TBLK = 16
FF_CHUNK = 1024


def _dot(a, b):
    return jnp.dot(a, b, preferred_element_type=F32)


def _dot_nt(a, b):
    return lax.dot_general(a, b, (((1,), (1,)), ((), ())), preferred_element_type=F32)


def _layer_spec(arr, l, single_buffer=False):
    mode = dict(pipeline_mode=pl.Buffered(1)) if single_buffer else {}
    return pl.BlockSpec((None,) + arr.shape[1:], lambda i: (l,) + (0,) * (arr.ndim - 1), **mode)


def _rope_table_kernel(pos_ref, freq_ref, sign_ref, cos_ref, sin_ref):
    ang = pos_ref[...] * freq_ref[...]
    cos_ref[...] = jnp.cos(ang)
    sin_ref[...] = jnp.sin(ang) * sign_ref[...]


def _rope_tables(pos_col, freq_row, sign_row):
    t = pos_col.shape[0]
    tm = 2048
    row = pl.BlockSpec((1, LANES), lambda i: (0, 0))
    tab = pl.BlockSpec((tm, LANES), lambda i: (i, 0))
    return pl.pallas_call(
        _rope_table_kernel,
        grid=(t // tm,),
        in_specs=[pl.BlockSpec((tm, 1), lambda i: (i, 0)), row, row],
        out_specs=[tab, tab],
        out_shape=[jax.ShapeDtypeStruct((t, LANES), F32)] * 2,
    )(pos_col, freq_row, sign_row)


def _proj_kernel(h_ref, w_ref, cos_ref, sin_ref, q_ref, k_ref, v_ref, u_ref, *, aw):
    nb, tb, d = h_ref.shape
    rows = nb * tb
    hb = h_ref[...].reshape(rows, d).astype(BF16)
    cos = cos_ref[...]
    sin = sin_ref[...]
    lane = lax.broadcasted_iota(jnp.int32, (1, LANES), 1)
    first_half = (lane % HEAD_DIM) < (HEAD_DIM // 2)

    def rope(x):
        rot = jnp.where(first_half,
                        pltpu.roll(x, LANES - HEAD_DIM // 2, 1),
                        pltpu.roll(x, HEAD_DIM // 2, 1))
        return x * cos + rot * sin

    def put(ref, j, val):
        ref[:, j] = val.astype(BF16).reshape(nb, tb, LANES)

    q = _dot(hb, w_ref[:, 0:aw])
    k = _dot(hb, w_ref[:, aw:2 * aw])
    v = _dot(hb, w_ref[:, 2 * aw:3 * aw])
    scale = HEAD_DIM ** -0.5 * LOG2E
    for j in range(aw // LANES):
        sl = slice(j * LANES, (j + 1) * LANES)
        put(q_ref, j, rope(q[:, sl]) * scale)
        put(k_ref, j, rope(k[:, sl]))
        put(v_ref, j, v[:, sl])
    u_ref[...] = _dot(hb, w_ref[:, 3 * aw:]).astype(BF16).reshape(u_ref.shape)


def _proj(h, h_spec, w_in, l, cos_t, sin_t, aw, batch, n_tblk):
    d, n = w_in.shape[1:]
    sw = n - 3 * aw
    rows = batch * TBLK
    tab = pl.BlockSpec((rows, LANES), lambda i: (i, 0))
    pairs = aw // LANES
    qkv_spec = pl.BlockSpec((batch, pairs, TBLK, LANES), lambda i: (0, 0, i, 0))
    qkv_shape = jax.ShapeDtypeStruct((batch, pairs, n_tblk * TBLK, LANES), BF16)
    return pl.pallas_call(
        functools.partial(_proj_kernel, aw=aw),
        grid=(n_tblk,),
        in_specs=[h_spec, _layer_spec(w_in, l), tab, tab],
        out_specs=[qkv_spec] * 3 + [pl.BlockSpec((None, batch, TBLK, sw), lambda i: (i, 0, 0, 0))],
        out_shape=[qkv_shape] * 3 + [jax.ShapeDtypeStruct((n_tblk, batch, TBLK, sw), BF16)],
        compiler_params=pltpu.CompilerParams(vmem_limit_bytes=VMEM_LIMIT),
    )(h, w_in, cos_t, sin_t)


def _attn_kernel(q_ref, k_ref, v_ref, o_ref,
                 xf, x4, q12, k12, v12, q3, k3, v3, acc12, den12, mx12, acc3, den3, mx3, onat,
                 bias2, bias_c, s_all, p_all):
    seq = q_ref.shape[0]
    n4 = seq // N_DIL4
    pad_blk = QBLK + n4
    base2 = QBLK + seq

    head0 = lax.broadcasted_iota(jnp.int32, (1, LANES), 1) < HEAD_DIM
    ones_h0 = jnp.where(head0, 1.0, 0.0).astype(BF16)
    ones_h1 = jnp.where(head0, 0.0, 1.0).astype(BF16)

    qi = lax.broadcasted_iota(jnp.int32, (QBLK, 2 * QBLK), 0)
    kj = lax.broadcasted_iota(jnp.int32, (QBLK, 2 * QBLK), 1)
    band = (kj >= qi) & (kj <= qi + QBLK)
    bias2[0] = jnp.where(band, 0.0, NEG_INF).astype(F32)
    bias2[1] = jnp.where(band & (kj >= QBLK), 0.0, NEG_INF).astype(F32)
    qi = lax.broadcasted_iota(jnp.int32, (QBLK, QBLK), 0)
    kj = lax.broadcasted_iota(jnp.int32, (QBLK, QBLK), 1)
    bias_c[...] = jnp.where(kj <= qi, 0.0, NEG_INF).astype(F32)

    nat = [ref[...] for ref in (q_ref, k_ref, v_ref)]
    for i in range(3):
        xf[i] = nat[i].astype(F32)
    for r in range(N_DIL4):
        for i in range(3):
            x4[i, r * n4:(r + 1) * n4, :] = xf[i, pl.ds(r, n4, stride=N_DIL4), :]
    q12[0:seq, :] = nat[0]
    zero_blk = jnp.zeros((QBLK, LANES), BF16)
    for dst, i in ((k12, 1), (v12, 2)):
        dst[0:QBLK, :] = zero_blk
        dst[QBLK:base2, :] = nat[i]
        for r in range(N_DIL4):
            b = base2 + r * pad_blk
            dst[b:b + QBLK, :] = zero_blk
            dst[b + QBLK:b + pad_blk, :] = x4[i, r * n4:(r + 1) * n4, :].astype(BF16)
    for r in range(N_DIL4):
        q12[seq + r * n4:seq + (r + 1) * n4, :] = x4[0, r * n4:(r + 1) * n4, :].astype(BF16)
    for r16 in range(N_DIL16):
        rows = pl.ds((r16 % N_DIL4) * n4 + r16 // N_DIL4, QBLK, stride=N_DIL4)
        for dst, i in ((q3, 0), (k3, 1), (v3, 2)):
            dst[r16 * QBLK:(r16 + 1) * QBLK, :] = x4[i, rows, :].astype(BF16)

    def stage_a(s_buf, nk, qb, kb):
        zero = jnp.zeros_like(kb)
        kblk = jnp.concatenate([jnp.where(head0, kb, zero), jnp.where(head0, zero, kb)], axis=0)
        s_buf[:, 0:2 * nk] = _dot_nt(qb, kblk)

    def stage_b(s_buf, p_buf, nk, bias):
        ms = [jnp.max(s_buf[:, a * nk:(a + 1) * nk] + bias, axis=-1, keepdims=True)
              for a in range(2)]
        for a in range(2):
            shifted = (s_buf[:, a * nk:(a + 1) * nk] - ms[a]) + bias
            p_buf[:, a * nk:(a + 1) * nk] = jnp.exp2(shifted).astype(BF16)
        return jnp.where(head0, jnp.broadcast_to(ms[0], (QBLK, LANES)),
                         jnp.broadcast_to(ms[1], (QBLK, LANES)))

    def stage_c(p_buf, nk, vb):
        zero = jnp.zeros_like(vb)
        w = jnp.concatenate(
            [jnp.concatenate([jnp.where(head0, vb, zero), jnp.broadcast_to(ones_h0, vb.shape)], axis=1),
             jnp.concatenate([jnp.where(head0, zero, vb), jnp.broadcast_to(ones_h1, vb.shape)], axis=1)],
            axis=0)
        r = _dot(p_buf[:, 0:2 * nk], w)
        return r[:, :LANES], r[:, LANES:]

    def run_pipeline(n_units, a_fn, bias_fn, max_fn, c_fn, nk):
        def a(u):
            u = jnp.int32(u)
            a_fn(u, s_all.at[u % SLOTS])

        def b(u):
            u = jnp.int32(u)
            slot = u % SLOTS
            max_fn(u, stage_b(s_all.at[slot], p_all.at[slot], nk, bias_fn(u)))

        def c(u):
            u = jnp.int32(u)
            slot = u % SLOTS
            c_fn(u, p_all.at[slot])

        g = GROUP
        for u in range(2 * g):
            a(u)
        for u in range(g):
            b(u)

        def body(j, carry):
            u0 = g * j
            for i in range(g):
                c(u0 + i)
            for i in range(g):
                b(u0 + g + i)
            for i in range(g):
                a(u0 + 2 * g + i)
            return carry

        lax.fori_loop(0, n_units // g - 2, body, 0)
        for u in range(n_units - 2 * g, n_units - g):
            c(u)
        for u in range(n_units - g, n_units):
            b(u)
        for u in range(n_units - g, n_units):
            c(u)

    blocks1 = seq // QBLK
    blocks4 = n4 // QBLK

    def offsets12(u):
        v = u - blocks1
        r = v // blocks4
        blk = v % blocks4
        is_nat = u < blocks1
        q_off = jnp.where(is_nat, u * QBLK, seq + v * QBLK)
        k_off = jnp.where(is_nat, u * QBLK, base2 + r * pad_blk + blk * QBLK)
        first = jnp.where(is_nat, u == 0, blk == 0)
        return pl.multiple_of(q_off, QBLK), pl.multiple_of(k_off, QBLK), first.astype(jnp.int32)

    def a12(u, s_buf):
        q_off, k_off, _ = offsets12(u)
        stage_a(s_buf, 2 * QBLK, q12[pl.ds(q_off, QBLK), :], k12[pl.ds(k_off, 2 * QBLK), :])

    def bias12(u):
        return bias2[offsets12(u)[2]]

    def max12(u, m):
        mx12[pl.ds(offsets12(u)[0], QBLK), :] = m

    def c12(u, p_buf):
        q_off, k_off, _ = offsets12(u)
        acc, den = stage_c(p_buf, 2 * QBLK, v12[pl.ds(k_off, 2 * QBLK), :])
        acc12[pl.ds(q_off, QBLK), :] = acc
        den12[pl.ds(q_off, QBLK), :] = den

    run_pipeline(blocks1 + N_DIL4 * blocks4, a12, bias12, max12, c12, 2 * QBLK)

    def rows3(u):
        return pl.ds(pl.multiple_of(u * QBLK, QBLK), QBLK)

    def a3(u, s_buf):
        stage_a(s_buf, QBLK, q3[rows3(u), :], k3[rows3(u), :])

    def dst3(u):
        return pl.ds((u % N_DIL4) * n4 + u // N_DIL4, QBLK, stride=N_DIL4)

    def max3(u, m):
        mx3[dst3(u), :] = m

    def c3(u, p_buf):
        acc, den = stage_c(p_buf, QBLK, v3[rows3(u), :])
        acc3[dst3(u), :] = acc
        den3[dst3(u), :] = den

    run_pipeline(N_DIL16, a3, lambda u: bias_c[...], max3, c3, QBLK)

    def merge(idx, carry):
        r = idx // blocks4
        c = idx % blocks4
        rn = pl.ds(c * QBLK * N_DIL4 + r, QBLK, stride=N_DIL4)
        r3 = pl.ds(pl.multiple_of(r * n4 + c * QBLK, QBLK), QBLK)
        r2 = pl.ds(pl.multiple_of(seq + r * n4 + c * QBLK, QBLK), QBLK)
        ma, mb, mc = mx12[rn, :], mx12[r2, :], mx3[r3, :]
        m = jnp.maximum(jnp.maximum(ma, mb), mc)
        wa, wb, wc = jnp.exp2(ma - m), jnp.exp2(mb - m), jnp.exp2(mc - m)
        num = wa * acc12[rn, :] + wb * acc12[r2, :] + wc * acc3[r3, :]
        den = wa * den12[rn, :] + wb * den12[r2, :] + wc * den3[r3, :]
        onat[rn, :] = num / den
        return carry

    lax.fori_loop(0, N_DIL4 * blocks4, merge, 0)
    o_ref[...] = onat[...].astype(o_ref.dtype)


def _attention(q, k, v):
    batch, pairs, seq, _ = q.shape
    assert seq == N_DIL16 * QBLK and DILATION_PAIRS == ((128, 1), (512, 4), (2048, 16))
    blk = pl.BlockSpec((None, None, seq, LANES), lambda j: (j // pairs, j % pairs, 0, 0))
    n4 = seq // N_DIL4
    pad_rows = QBLK + seq + N_DIL4 * (QBLK + n4)
    vm = pltpu.VMEM
    scratch = [
        vm((3, seq, LANES), F32), vm((3, seq, LANES), F32),
        vm((2 * seq, LANES), BF16), vm((pad_rows, LANES), BF16), vm((pad_rows, LANES), BF16),
        vm((seq, LANES), BF16), vm((seq, LANES), BF16), vm((seq, LANES), BF16),
        vm((2 * seq, LANES), F32), vm((2 * seq, LANES), F32), vm((2 * seq, LANES), F32),
        vm((seq, LANES), F32), vm((seq, LANES), F32), vm((seq, LANES), F32),
        vm((seq, LANES), F32),
        vm((2, QBLK, 2 * QBLK), F32), vm((QBLK, QBLK), F32),
        vm((SLOTS, QBLK, 4 * QBLK), F32), vm((SLOTS, QBLK, 4 * QBLK), BF16),
    ]
    return pl.pallas_call(
        _attn_kernel,
        grid=(batch * pairs,),
        in_specs=[blk, blk, blk],
        out_specs=blk,
        out_shape=jax.ShapeDtypeStruct(q.shape, BF16),
        scratch_shapes=scratch,
        compiler_params=pltpu.CompilerParams(vmem_limit_bytes=VMEM_LIMIT),
    )(q, k, v)


def _ssm_kernel(u0_ref, un_ref, perm_ref, permt_ref, bh_ref, ch_ref, lam_ref, d_ref, wg_ref, bg_ref,
                gain_ref, o_ref, up0, up1, bu0, bu1, xs0, xs1, ys0, ys1, ya, gt, zb, st):
    batch, tb, sw = un_ref.shape
    half_states = bh_ref.shape[2] // 2
    half_ch = bh_ref.shape[1]
    cchunk = 256
    g = pl.program_id(0)

    chunks = [(h, c) for h in range(2) for c in range(0, half_states, cchunk)]

    def load_block(u_ref, up):
        up[...] = _dot(perm_ref[...], u_ref[...].reshape(batch * tb, sw)).astype(BF16)

    def expand_cols(up, bu, h, c):
        lhs = up[:, h * half_ch:(h + 1) * half_ch]
        for part in range(2):
            lo = part * half_states + c
            bu[:, h * 2 * half_states + lo:h * 2 * half_states + lo + cchunk] = _dot(
                lhs, bh_ref[h, :, lo:lo + cchunk])

    def scan_cols(bu, xs, h, c):
        for sub in range(c, c + cchunk, LANES):
            cre = slice(h * 2 * half_states + sub, h * 2 * half_states + sub + LANES)
            cim = slice(cre.start + half_states, cre.stop + half_states)
            cl = slice(h * half_states + sub, h * half_states + sub + LANES)
            lr = lam_ref[0:1, cl]
            li = lam_ref[1:2, cl]
            xr = st[:, cre]
            xi = st[:, cim]
            for i in range(tb):
                r = slice(i * batch, (i + 1) * batch)
                xr, xi = (lr * xr - li * xi + bu[r, cre], lr * xi + li * xr + bu[r, cim])
                xs[r, cre] = xr.astype(BF16)
                xs[r, cim] = xi.astype(BF16)
            st[:, cre] = xr
            st[:, cim] = xi

    def project_cols(xs, ys, h, c):
        out = slice(h * half_ch, (h + 1) * half_ch)
        acc = ys[:, out]
        for part in range(2):
            lo = part * half_states + c
            acc = acc + _dot(xs[:, h * 2 * half_states + lo:h * 2 * half_states + lo + cchunk],
                             ch_ref[h, lo:lo + cchunk, :])
        ys[:, out] = acc

    pieces = len(chunks) // 2
    piece_rows = batch * tb // pieces

    def act_rows(ys, i):
        r = slice(i * piece_rows, (i + 1) * piece_rows)
        ya[r, :] = jax.nn.gelu(ys[r, :])

    def gate_all():
        gt[...] = _dot(ya[...].astype(BF16), wg_ref[...]) + bg_ref[...]

    def norm_rows(i):
        r = slice(i * piece_rows, (i + 1) * piece_rows)
        z = ya[r, :] * (1.0 / (1.0 + jnp.exp(-gt[r, :])))
        z = z * lax.rsqrt(jnp.mean(z * z, axis=-1, keepdims=True) + RMS_EPS) * gain_ref[...]
        zb[r, :] = z.astype(BF16)

    def step(xs_q, ys_q, up_p, bu_p, xs_p, ys_p, up_q, bu_q):
        load_block(un_ref, up_q)
        for idx, (h, c) in enumerate(chunks):
            if idx < pieces:
                act_rows(ys_p, idx)
            else:
                if idx == pieces:
                    gate_all()
                    ys_p[...] = d_ref[...] * up_p[...].astype(F32)
                norm_rows(idx - pieces)
            project_cols(xs_q, ys_q, h, c)
            scan_cols(bu_p, xs_p, h, c)
            expand_cols(up_q, bu_q, h, c)
        o_ref[...] = _dot(permt_ref[...], zb[...]).astype(o_ref.dtype)

    @pl.when(g == 0)
    def _():
        st[...] = jnp.zeros_like(st)
        xs1[...] = jnp.zeros_like(xs1)
        ys0[...] = jnp.zeros_like(ys0)
        ys1[...] = jnp.zeros_like(ys1)
        load_block(u0_ref, up0)
        for h, c in chunks:
            expand_cols(up0, bu0, h, c)

    @pl.when(g % 2 == 0)
    def _():
        step(xs1, ys1, up0, bu0, xs0, ys0, up1, bu1)

    @pl.when(g % 2 == 1)
    def _():
        step(xs0, ys0, up1, bu1, xs1, ys1, up0, bu0)


def _ssm(u, perm, perm_t, l, bh, ch, lam, d_skip, w_glu, b_glu, gain):
    n_tblk, batch, tb, sw = u.shape
    assert n_tblk % 2 == 0
    m = batch * tb
    n_state = bh.shape[1] * bh.shape[3]
    full = lambda a: pl.BlockSpec(a.shape, lambda i: (0,) * a.ndim, pipeline_mode=pl.Buffered(1))
    consts = (bh, ch, lam, d_skip, w_glu, b_glu, gain)
    last = n_tblk - 1
    vm = pltpu.VMEM
    return pl.pallas_call(
        _ssm_kernel,
        grid=(n_tblk + 2,),
        in_specs=[pl.BlockSpec((None, batch, tb, sw), lambda i: (0, 0, 0, 0)),
                  pl.BlockSpec((None, batch, tb, sw), lambda i: (jnp.minimum(i + 1, last), 0, 0, 0)),
                  full(perm), full(perm_t)] + [_layer_spec(a, l, single_buffer=True) for a in consts],
        out_specs=pl.BlockSpec((m, sw), lambda i: (jnp.maximum(i - 2, 0), 0)),
        out_shape=jax.ShapeDtypeStruct((n_tblk * m, sw), BF16),
        scratch_shapes=[vm((m, sw), BF16), vm((m, sw), BF16),
                        vm((m, n_state), F32), vm((m, n_state), F32),
                        vm((m, n_state), BF16), vm((m, n_state), BF16),
                        vm((m, sw), F32), vm((m, sw), F32),
                        vm((m, sw), F32), vm((m, sw), F32), vm((m, sw), BF16),
                        vm((batch, n_state), F32)],
        compiler_params=pltpu.CompilerParams(dimension_semantics=("arbitrary",),
                                             vmem_limit_bytes=VMEM_LIMIT),
    )(u, u, perm, perm_t, *consts)


def _ssm_params(a_re, a_im, log_dt, b_re, b_im, c_re, c_im):
    depth, g, p = a_re.shape
    n = b_re.shape[-1]
    dt = jnp.exp(log_dt)[..., None]
    mag = jnp.exp(a_re * dt)
    ang = a_im * dt
    lb_re = mag * jnp.cos(ang)
    lb_im = mag * jnp.sin(ang)
    den = a_re * a_re + a_im * a_im
    nr = lb_re - 1.0
    ni = lb_im
    cr = (nr * a_re + ni * a_im) / den
    ci = (ni * a_re - nr * a_im) / den
    bb_re = cr[..., None] * b_re - ci[..., None] * b_im
    bb_im = cr[..., None] * b_im + ci[..., None] * b_re
    gh = g // 2
    eye = jnp.eye(gh, dtype=F32)
    halves = (slice(0, gh), slice(gh, g))

    def pack_b(bb):
        return jnp.einsum('lgpn,gh->lgnhp', bb, eye).reshape(depth, gh * n, gh * p)

    def pack_c(c):
        return jnp.einsum('lgnp,gh->lgphn', c, eye).reshape(depth, gh * p, gh * n)

    bh = jnp.stack([jnp.concatenate([pack_b(bb_re[:, s]), pack_b(bb_im[:, s])], axis=2)
                    for s in halves], axis=1)
    ch = jnp.stack([jnp.concatenate([pack_c(c_re[:, s]), -pack_c(c_im[:, s])], axis=1)
                    for s in halves], axis=1)
    lam = jnp.stack([lb_re.reshape(depth, -1), lb_im.reshape(depth, -1)], axis=1)
    return bh.astype(BF16), ch.astype(BF16), lam


def _layer_norm(x, g, b):
    mu = jnp.mean(x, axis=-1, keepdims=True)
    xc = x - mu
    var = jnp.mean(xc * xc, axis=-1, keepdims=True)
    return xc * lax.rsqrt(var + LN_EPS) * g + b


def _post_kernel(h_ref, a_ref, s_ref, ag_ref, wo_ref, bo_ref, g1_ref, b1_ref,
                 w1_ref, bf1_ref, w2_ref, bf2_ref, g2_ref, b2_ref, o_ref, *, alpha):
    nb, tb, d = h_ref.shape
    rows = nb * tb
    pairs = a_ref.shape[1]
    aw = pairs * LANES
    a = jnp.concatenate([a_ref[:, j].reshape(rows, LANES) for j in range(pairs)], axis=1).astype(F32)
    a = a * lax.rsqrt(jnp.mean(a * a, axis=-1, keepdims=True) + RMS_EPS) * ag_ref[...]
    mix = (_dot(a.astype(BF16), wo_ref[0:aw, :]) + _dot(s_ref[...], wo_ref[aw:, :]) + bo_ref[...])
    h1 = _layer_norm(alpha * h_ref[...].reshape(rows, d) + mix, g1_ref[...], b1_ref[...])
    h1b = h1.astype(BF16)
    ff = jnp.zeros_like(h1)
    for c in range(0, w1_ref.shape[1], FF_CHUNK):
        act = jnp.maximum(_dot(h1b, w1_ref[:, c:c + FF_CHUNK]) + bf1_ref[:, c:c + FF_CHUNK], 0.0)
        ff = ff + _dot((act * act).astype(BF16), w2_ref[c:c + FF_CHUNK, :])
    out = _layer_norm(alpha * h1 + ff + bf2_ref[...], g2_ref[...], b2_ref[...])
    o_ref[...] = out.reshape(nb, tb, d)


def _post(h, h_spec, attn, ssm, l, consts, alpha, out_shape, out_spec):
    batch, pairs, seq, _ = attn.shape
    n_tblk = seq // TBLK
    rows = batch * TBLK
    return pl.pallas_call(
        functools.partial(_post_kernel, alpha=alpha),
        grid=(n_tblk,),
        in_specs=[h_spec, pl.BlockSpec((batch, pairs, TBLK, LANES), lambda i: (0, 0, i, 0)),
                  pl.BlockSpec((rows, ssm.shape[1]), lambda i: (i, 0))]
                 + [_layer_spec(a, l, single_buffer=True) for a in consts],
        out_specs=out_spec,
        out_shape=out_shape,
        compiler_params=pltpu.CompilerParams(vmem_limit_bytes=VMEM_LIMIT),
    )(h, attn, ssm, *consts)


def kernel(x, positions, w_in, attn_gain, ssm_gain, ssm_a_re, ssm_a_im, ssm_log_dt, ssm_b_re,
           ssm_b_im, ssm_c_re, ssm_c_im, ssm_d, w_glu, b_glu, w_out, b_out, ln1_g, ln1_b,
           w_ff1, b_ff1, w_ff2, b_ff2, ln2_g, ln2_b):
    batch, seq, d = x.shape
    depth = w_in.shape[0]
    aw = attn_gain.shape[1]
    t = batch * seq
    n_tblk = seq // TBLK
    rows = batch * TBLK
    alpha = (2.0 * depth) ** 0.25
    vec = lambda a: a.reshape(depth, 1, -1).astype(F32)

    half = HEAD_DIM // 2
    inv_freq = ROPE_THETA ** (-jnp.arange(half, dtype=F32) * 2.0 / HEAD_DIM)
    freq_row = jnp.tile(inv_freq, LANES // half).reshape(1, LANES)
    sign_row = jnp.tile(jnp.concatenate([-jnp.ones(half, F32), jnp.ones(half, F32)]),
                        LANES // HEAD_DIM).reshape(1, LANES)
    pos_col = positions.reshape(batch, n_tblk, TBLK).transpose(1, 0, 2).reshape(t, 1).astype(F32)
    cos_t, sin_t = _rope_tables(pos_col, freq_row, sign_row)

    r_out = jnp.arange(rows, dtype=jnp.int32)
    r_in = (r_out % batch) * TBLK + r_out // batch
    perm = (r_in[:, None] == jnp.arange(rows, dtype=jnp.int32)[None, :]).astype(BF16)
    perm_t = perm.T

    w_in_b = w_in.astype(BF16)
    bh, ch, lam = _ssm_params(ssm_a_re, ssm_a_im, ssm_log_dt, ssm_b_re, ssm_b_im, ssm_c_re, ssm_c_im)
    ssm_consts = (bh, ch, lam, vec(ssm_d), w_glu.astype(BF16), vec(b_glu), vec(ssm_gain))
    post_consts = (vec(attn_gain), w_out.astype(BF16), vec(b_out), vec(ln1_g), vec(ln1_b),
                   w_ff1.astype(BF16), vec(b_ff1), w_ff2.astype(BF16), vec(b_ff2), vec(ln2_g),
                   vec(ln2_b))

    by_seq_spec = pl.BlockSpec((batch, None, TBLK, d), lambda i: (0, i, 0, 0))
    by_blk_spec = pl.BlockSpec((batch, TBLK, d), lambda i: (i, 0, 0))
    h = x.reshape(batch, n_tblk, TBLK, d)
    h_spec = by_seq_spec
    for l in range(depth):
        q, k, v, u = _proj(h, h_spec, w_in_b, l, cos_t, sin_t, aw, batch, n_tblk)
        attn = _attention(q, k, v)
        ssm = _ssm(u, perm, perm_t, l, *ssm_consts)
        if l == depth - 1:
            out_shape, out_spec = jax.ShapeDtypeStruct((batch, n_tblk, TBLK, d), F32), by_seq_spec
        else:
            out_shape, out_spec = jax.ShapeDtypeStruct((n_tblk * batch, TBLK, d), F32), by_blk_spec
        h = _post(h, h_spec, attn, ssm, l, post_consts, alpha, out_shape, out_spec)
        h_spec = by_blk_spec
    return h.reshape(batch, seq, d)
```
